```python
import math
import jax, jax.numpy as jnp
from jax import lax
import numpy as np

D_MODEL = 2048
BATCH = 2
SEQ = 16384
DEPTH = 1
DEC_BATCH = 8
DEC_SEQ = 64
PAST_LEN = 1024

CHUNK = 64
Q_BLOCK = 128
N_HEADS = 8
N_KV_HEADS = 2
HEAD_DIM = 128
GROUP = N_HEADS // N_KV_HEADS
ATTN_WIDTH = N_HEADS * HEAD_DIM
IDX_HEADS = 16
IDX_DIM = 64
TOPK_MAX = 256
GLA_HEADS = 4
GLA_DK = 128
GLA_DV = 256
GLA_KW = GLA_HEADS * GLA_DK
GLA_VW = GLA_HEADS * GLA_DV
GLA_RANK = 16
GLA_NORMALIZER = 16.0
MIX_WIDTH = ATTN_WIDTH + GLA_VW
N_MEM = 256
MEM_HEADS = 4
MEM_HEAD_DIM = 128
MEM_WIDTH = MEM_HEADS * MEM_HEAD_DIM
D_FF = 5632
N_BUCKETS = 32
MAX_DISTANCE = 128
DN_ALPHA = (2.0 * DEPTH) ** 0.25
DN_BETA = (8.0 * DEPTH) ** -0.25
LN_EPS = 1e-5
_SPLITS = (ATTN_WIDTH, N_KV_HEADS * HEAD_DIM, N_KV_HEADS * HEAD_DIM, IDX_HEADS * IDX_DIM, IDX_DIM, IDX_HEADS,
           GLA_KW, GLA_KW, GLA_VW, GLA_RANK, GLA_VW)
IN_WIDTH = (ATTN_WIDTH + 2 * N_KV_HEADS * HEAD_DIM + IDX_HEADS * IDX_DIM + IDX_DIM + IDX_HEADS
            + 2 * GLA_KW + GLA_VW + GLA_RANK + GLA_VW)

kernel_name = 'hybrid_dsa_gla_streaming_encoder_step'

F32 = jnp.float32


def post_norm(x, sub, g, b):
    h = DN_ALPHA * x.astype(F32) + sub.astype(F32)
    mu = h.mean(-1, keepdims=True)
    var = jnp.square(h - mu).mean(-1, keepdims=True)
    return ((h - mu) * lax.rsqrt(var + LN_EPS) * g.astype(F32) + b.astype(F32)).astype(x.dtype)


def swiglu(x, wg, wu, wd):
    return (jax.nn.silu(x @ wg) * (x @ wu)) @ wd


def rel_bucket(rel):
    half = N_BUCKETS // 2
    max_exact = half // 2
    ret = jnp.where(rel > 0, half, 0)
    n = jnp.abs(rel)
    nf = jnp.maximum(n, 1).astype(F32)
    large = max_exact + (jnp.log(nf / max_exact) / math.log(MAX_DISTANCE / max_exact)
                         * (half - max_exact)).astype(jnp.int32)
    large = jnp.minimum(large, half - 1)
    return ret + jnp.where(n < max_exact, n, large)


def mixer_project(x, w_in, w_a2, b_a):
    B, T = x.shape[:2]
    offs = np.cumsum(_SPLITS)[:-1].tolist()
    q, k, v, qi, ki, wi, gq, gk, gv, glr, gr = jnp.split(x @ w_in, offs, axis=-1)
    q = q.reshape(B, T, N_HEADS, HEAD_DIM)
    k = k.reshape(B, T, N_KV_HEADS, HEAD_DIM)
    v = v.reshape(B, T, N_KV_HEADS, HEAD_DIM)
    qi = qi.reshape(B, T, IDX_HEADS, IDX_DIM)
    gq = gq.reshape(B, T, GLA_HEADS, GLA_DK) * GLA_DK ** -0.5
    gk = gk.reshape(B, T, GLA_HEADS, GLA_DK)
    gv = gv.reshape(B, T, GLA_HEADS, GLA_DV)
    la = (jax.nn.log_sigmoid((glr @ w_a2 + b_a).astype(F32)) / GLA_NORMALIZER).reshape(B, T, GLA_HEADS, GLA_DK)
    return q, k, v, qi, ki, wi, gq, gk, gv, la, gr


def dsa_block(q, qi, wi, qpos, k_all, v_all, ki_all, n_sel, rel_bias):
    B, Tb = q.shape[:2]
    L = k_all.shape[1]
    limit = (qpos // CHUNK + 1) * CHUNK
    kpos = jnp.arange(L, dtype=jnp.int32)
    s = jnp.einsum('bthd,bsd->bths', qi.astype(F32), ki_all.astype(F32)) * IDX_DIM ** -0.5
    score = jnp.einsum('bths,bth->bts', jax.nn.relu(s), wi.astype(F32)) * IDX_HEADS ** -0.5
    score = jnp.where(kpos[None, None, :] < limit[None, :, None], score, -jnp.inf)
    _, idx = lax.top_k(score, n_sel)
    gather = jax.vmap(lambda rows, ii: rows[ii])
    k_sel = gather(k_all, idx)
    v_sel = gather(v_all, idx)
    valid = idx < limit[None, :, None]
    bias = rel_bias[rel_bucket(idx - qpos[None, :, None])]
    bias = bias.reshape(B, Tb, n_sel, N_KV_HEADS, GROUP).transpose(0, 1, 3, 4, 2)
    qg = q.reshape(B, Tb, N_KV_HEADS, GROUP, HEAD_DIM)
    logits = jnp.einsum('btkgd,btnkd->btkgn', qg, k_sel).astype(F32) * HEAD_DIM ** -0.5 + bias.astype(F32)
    logits = jnp.where(valid[:, :, None, None, :], logits, -jnp.inf)
    p = jax.nn.softmax(logits, axis=-1).astype(v_sel.dtype)
    o = jnp.einsum('btkgn,btnkd->btkgd', p, v_sel)
    return o.reshape(B, Tb, ATTN_WIDTH)


def dsa_prompt(q, qi, wi, k, v, ki, rel_bias):
    B, T = q.shape[:2]
    nb = T // Q_BLOCK
    n_sel = min(TOPK_MAX, T // 4)

    def blk(a):
        return jnp.moveaxis(a.reshape(B, nb, Q_BLOCK, *a.shape[2:]), 1, 0)

    pos = jnp.arange(T, dtype=jnp.int32).reshape(nb, Q_BLOCK)
    out = lax.map(lambda a: dsa_block(a[0], a[1], a[2], a[3], k, v, ki, n_sel, rel_bias),
                  (blk(q), blk(qi), blk(wi), pos))
    return jnp.moveaxis(out, 0, 1).reshape(B, T, ATTN_WIDTH)


def gla_chunked(q, k, v, la, s0, chunk):
    B, T = q.shape[:2]
    n = T // chunk

    def chunks(a):
        return jnp.moveaxis(a.astype(F32).reshape(B, n, chunk, *a.shape[2:]), 1, 0)

    causal = jnp.tril(jnp.ones((chunk, chunk), dtype=bool))

    def step(S, inp):
        qc, kc, vc, lc = inp
        b = jnp.cumsum(lc, axis=1)
        o_inter = jnp.einsum('bihk,bhkv->bihv', qc * jnp.exp(b), S)
        diff = b[:, :, None] - b[:, None, :]
        decay = jnp.exp(jnp.where(causal[None, :, :, None, None], diff, -jnp.inf))
        A = jnp.einsum('bihk,bjhk,bijhk->bijh', qc, kc, decay)
        o_intra = jnp.einsum('bijh,bjhv->bihv', A, vc)
        b_last = b[:, -1]
        S = jnp.exp(b_last)[..., None] * S + jnp.einsum('bjhk,bjhv->bhkv', kc * jnp.exp(b_last[:, None] - b), vc)
        return S, o_inter + o_intra

    S, o = lax.scan(step, s0.astype(F32), (chunks(q), chunks(k), chunks(v), chunks(la)))
    return jnp.moveaxis(o, 0, 1).reshape(B, T, GLA_HEADS, GLA_DV), S


def gla_output(o, gr, g):
    B, T = o.shape[:2]
    mu = o.mean(-1, keepdims=True)
    var = jnp.square(o - mu).mean(-1, keepdims=True)
    o = (o - mu) * lax.rsqrt(var + LN_EPS) * g.astype(F32).reshape(GLA_HEADS, GLA_DV)
    return (o.reshape(B, T, GLA_VW) * jax.nn.silu(gr.astype(F32))).astype(gr.dtype)


def mem_kv(mem, w_mk, w_mv):
    B, N = mem.shape[:2]
    return ((mem @ w_mk).reshape(B, N, MEM_HEADS, MEM_HEAD_DIM),
            (mem @ w_mv).reshape(B, N, MEM_HEADS, MEM_HEAD_DIM))


def mem_attend(x, mk, mv, w_mq, w_mo):
    B, T = x.shape[:2]
    q = (x @ w_mq).reshape(B, T, MEM_HEADS, MEM_HEAD_DIM)
    logits = jnp.einsum('bthd,bnhd->bthn', q, mk.astype(q.dtype)).astype(F32) * MEM_HEAD_DIM ** -0.5
    p = jax.nn.softmax(logits, axis=-1).astype(x.dtype)
    o = jnp.einsum('bthn,bnhd->bthd', p, mv.astype(x.dtype)).reshape(B, T, MEM_WIDTH)
    return o @ w_mo


def setup_inputs(seed: int = 0) -> dict:
    key = jax.random.key(seed)
    ks = jax.random.split(key, 32)

    def nrm(k, shape, s):
        return jax.random.normal(k, shape, F32) * s

    return {
        'x_prompt': nrm(ks[0], (BATCH, SEQ, D_MODEL), 1.0),
        'x_sample': nrm(ks[1], (DEC_BATCH, DEC_SEQ, D_MODEL), 1.0),
        'cache_k': nrm(ks[2], (DEPTH, DEC_BATCH, PAST_LEN, N_KV_HEADS, HEAD_DIM), 1.0),
        'cache_v': nrm(ks[3], (DEPTH, DEC_BATCH, PAST_LEN, N_KV_HEADS, HEAD_DIM), 1.0),
        'cache_idx_k': nrm(ks[4], (DEPTH, DEC_BATCH, PAST_LEN, IDX_DIM), 1.0),
        'state_gla': nrm(ks[5], (DEPTH, DEC_BATCH, GLA_HEADS, GLA_DK, GLA_DV), 1.0),
        'cache_mem_k': nrm(ks[6], (DEPTH, DEC_BATCH, N_MEM, MEM_HEADS, MEM_HEAD_DIM), 1.0),
        'cache_mem_v': nrm(ks[7], (DEPTH, DEC_BATCH, N_MEM, MEM_HEADS, MEM_HEAD_DIM), 1.0),
        'mem_prompt': nrm(ks[8], (BATCH, N_MEM, D_MODEL), 1.0),
        'rel_bias': nrm(ks[9], (N_BUCKETS, N_HEADS), 0.5),
        'ln_g': 1.0 + nrm(ks[10], (DEPTH, 4, D_MODEL), 0.02),
        'ln_b': nrm(ks[11], (DEPTH, 4, D_MODEL), 0.02),
        'ffn1_wg': nrm(ks[12], (DEPTH, D_MODEL, D_FF), D_MODEL ** -0.5),
        'ffn1_wu': nrm(ks[13], (DEPTH, D_MODEL, D_FF), D_MODEL ** -0.5),
        'ffn1_wd': nrm(ks[14], (DEPTH, D_FF, D_MODEL), D_FF ** -0.5 * DN_BETA),
        'w_in': nrm(ks[15], (DEPTH, D_MODEL, IN_WIDTH), D_MODEL ** -0.5),
        'w_a2': nrm(ks[16], (DEPTH, GLA_RANK, GLA_KW), GLA_RANK ** -0.5),
        'b_a': nrm(ks[17], (DEPTH, GLA_KW), 0.1),
        'gla_norm_g': 1.0 + nrm(ks[18], (DEPTH, GLA_VW), 0.02),
        'w_o': nrm(ks[19], (DEPTH, MIX_WIDTH, D_MODEL), MIX_WIDTH ** -0.5 * DN_BETA),
        'w_mq': nrm(ks[20], (DEPTH, D_MODEL, MEM_WIDTH), D_MODEL ** -0.5),
        'w_mk': nrm(ks[21], (DEPTH, D_MODEL, MEM_WIDTH), D_MODEL ** -0.5),
        'w_mv': nrm(ks[22], (DEPTH, D_MODEL, MEM_WIDTH), D_MODEL ** -0.5),
        'w_mo': nrm(ks[23], (DEPTH, MEM_WIDTH, D_MODEL), MEM_WIDTH ** -0.5 * DN_BETA),
        'ffn2_wg': nrm(ks[24], (DEPTH, D_MODEL, D_FF), D_MODEL ** -0.5),
        'ffn2_wu': nrm(ks[25], (DEPTH, D_MODEL, D_FF), D_MODEL ** -0.5),
        'ffn2_wd': nrm(ks[26], (DEPTH, D_FF, D_MODEL), D_FF ** -0.5 * DN_BETA),
    }


def reference(x_prompt, x_sample, cache_k, cache_v, cache_idx_k, state_gla, cache_mem_k, cache_mem_v,
              mem_prompt, rel_bias, ln_g, ln_b, ffn1_wg, ffn1_wu, ffn1_wd, w_in, w_a2, b_a, gla_norm_g,
              w_o, w_mq, w_mk, w_mv, w_mo, ffn2_wg, ffn2_wu, ffn2_wd):
    xp, xs = x_prompt, x_sample
    Bp = xp.shape[0]
    Bs, Ts = xs.shape[:2]
    P = cache_k.shape[2]
    L_s = P + Ts
    n_sel_s = min(TOPK_MAX, L_s // 4)
    qpos_s = P + jnp.arange(Ts, dtype=jnp.int32)
    pk, pv, pki, pS, pmk, pmv = [], [], [], [], [], []
    sk, sv, ski, sS = [], [], [], []
    for l in range(DEPTH):
        xp = post_norm(xp, 0.5 * swiglu(xp, ffn1_wg[l], ffn1_wu[l], ffn1_wd[l]), ln_g[l, 0], ln_b[l, 0])
        xs = post_norm(xs, 0.5 * swiglu(xs, ffn1_wg[l], ffn1_wu[l], ffn1_wd[l]), ln_g[l, 0], ln_b[l, 0])

        q, k, v, qi, ki, wi, gq, gk, gv, la, gr = mixer_project(xp, w_in[l], w_a2[l], b_a[l])
        attn = dsa_prompt(q, qi, wi, k, v, ki, rel_bias)
        s0 = jnp.zeros((Bp, GLA_HEADS, GLA_DK, GLA_DV), F32)
        go, S_p = gla_chunked(gq, gk, gv, la, s0, CHUNK)
        mix = jnp.concatenate([attn, gla_output(go, gr, gla_norm_g[l])], axis=-1) @ w_o[l]
        xp = post_norm(xp, mix, ln_g[l, 1], ln_b[l, 1])
        pk.append(k); pv.append(v); pki.append(ki); pS.append(S_p.astype(state_gla.dtype))

        q, k, v, qi, ki, wi, gq, gk, gv, la, gr = mixer_project(xs, w_in[l], w_a2[l], b_a[l])
        k_all = jnp.concatenate([cache_k[l].astype(k.dtype), k], axis=1)
        v_all = jnp.concatenate([cache_v[l].astype(v.dtype), v], axis=1)
        ki_all = jnp.concatenate([cache_idx_k[l].astype(ki.dtype), ki], axis=1)
        attn = dsa_block(q, qi, wi, qpos_s, k_all, v_all, ki_all, n_sel_s, rel_bias)
        go, S_s = gla_chunked(gq, gk, gv, la, state_gla[l], Ts)
        mix = jnp.concatenate([attn, gla_output(go, gr, gla_norm_g[l])], axis=-1) @ w_o[l]
        xs = post_norm(xs, mix, ln_g[l, 1], ln_b[l, 1])
        sk.append(k); sv.append(v); ski.append(ki); sS.append(S_s.astype(state_gla.dtype))

        mk_p, mv_p = mem_kv(mem_prompt, w_mk[l], w_mv[l])
        xp = post_norm(xp, mem_attend(xp, mk_p, mv_p, w_mq[l], w_mo[l]), ln_g[l, 2], ln_b[l, 2])
        xs = post_norm(xs, mem_attend(xs, cache_mem_k[l], cache_mem_v[l], w_mq[l], w_mo[l]), ln_g[l, 2], ln_b[l, 2])
        pmk.append(mk_p); pmv.append(mv_p)

        xp = post_norm(xp, 0.5 * swiglu(xp, ffn2_wg[l], ffn2_wu[l], ffn2_wd[l]), ln_g[l, 3], ln_b[l, 3])
        xs = post_norm(xs, 0.5 * swiglu(xs, ffn2_wg[l], ffn2_wu[l], ffn2_wd[l]), ln_g[l, 3], ln_b[l, 3])

    return (xp, xs,
            jnp.stack(pk), jnp.stack(pv), jnp.stack(pki), jnp.stack(pS), jnp.stack(pmk), jnp.stack(pmv),
            jnp.stack(sk), jnp.stack(sv), jnp.stack(ski), jnp.stack(sS))
```

```python
import functools
import math

import jax
import jax.numpy as jnp
from jax import lax
from jax.experimental import pallas as pl
from jax.experimental.pallas import tpu as pltpu

F32 = jnp.float32
BF16 = jnp.bfloat16
I32 = jnp.int32

LANES = 128
SUBLANES = 8
VMEM_BYTES_V7X = 64 * 1024 * 1024
VMEM_LIMIT = VMEM_BYTES_V7X - 8 * 1024 * 1024

CHUNK = 64
N_HEADS = 8
N_KV_HEADS = 2
HEAD_DIM = 128
GROUP = N_HEADS // N_KV_HEADS
IDX_HEADS = 16
IDX_DIM = 64
TOPK_MAX = 256
GLA_HEADS = 4
GLA_DK = 128
GLA_DV = 256
GLA_RANK = 16
GLA_NORMALIZER = 16.0
MEM_HEADS = 4
MEM_HEAD_DIM = 128
N_BUCKETS = 32
MAX_DISTANCE = 128
LN_EPS = 1e-5
INT_MIN = -2 ** 31
NEG_BIG = -1e30
GLA_SUB = 16


def _dot(a, b):
    return jnp.dot(a, b, preferred_element_type=F32)


def _dot_nt(a, b):
    return lax.dot_general(a, b, (((1,), (1,)), ((), ())), preferred_element_type=F32)


def _params(semantics):
    return pltpu.CompilerParams(dimension_semantics=semantics, vmem_limit_bytes=VMEM_LIMIT)


def _post_norm(x, sub, g, b, alpha):
    h = alpha * x + sub
    mu = jnp.mean(h, axis=-1, keepdims=True)
    d = h - mu
    var = jnp.mean(d * d, axis=-1, keepdims=True)
    return d * lax.rsqrt(var + LN_EPS) * g + b


def _ffn_ln_kernel(x_ref, wg_ref, wu_ref, wd_ref, g_ref, b_ref, o_ref, xb_ref, acc_ref, *, alpha):
    j = pl.program_id(1)

    @pl.when(j == 0)
    def _():
        xb_ref[...] = x_ref[...].astype(BF16)
        acc_ref[...] = jnp.zeros_like(acc_ref)

    xb = xb_ref[...]
    hg = _dot(xb, wg_ref[...])
    hu = _dot(xb, wu_ref[...])
    h = (hg * jax.nn.sigmoid(hg)) * hu
    acc_ref[...] += _dot(h.astype(BF16), wd_ref[...])

    @pl.when(j == pl.num_programs(1) - 1)
    def _():
        o_ref[...] = _post_norm(x_ref[...], 0.5 * acc_ref[...], g_ref[...], b_ref[...], alpha)


def _ffn_ln(x, wg, wu, wd, g, b, alpha, tm=512, tf=512):
    n, d = x.shape
    f = wg.shape[1]
    tm = min(tm, n)
    assert n % tm == 0 and f % tf == 0
    return pl.pallas_call(
        functools.partial(_ffn_ln_kernel, alpha=alpha),
        grid=(n // tm, f // tf),
        in_specs=[
            pl.BlockSpec((tm, d), lambda i, j: (i, 0)),
            pl.BlockSpec((d, tf), lambda i, j: (0, j)),
            pl.BlockSpec((d, tf), lambda i, j: (0, j)),
            pl.BlockSpec((tf, d), lambda i, j: (j, 0)),
            pl.BlockSpec((1, d), lambda i, j: (0, 0)),
            pl.BlockSpec((1, d), lambda i, j: (0, 0)),
        ],
        out_specs=pl.BlockSpec((tm, d), lambda i, j: (i, 0)),
        out_shape=jax.ShapeDtypeStruct((n, d), F32),
        scratch_shapes=[pltpu.VMEM((tm, d), BF16), pltpu.VMEM((tm, d), F32)],
        compiler_params=_params(("parallel", "arbitrary")),
        name="ffn_ln",
    )(x, wg, wu, wd, g, b)


def _attn_proj_kernel(x_ref, wq_ref, wqi_ref, wkv_ref, wkw_ref,
                      q_ref, qi_ref, k_ref, v_ref, kb_ref, vb_ref, ki_ref, kilo_ref, kihi_ref, wi_ref):
    xb = x_ref[...].astype(BF16)
    q_ref[...] = _dot(xb, wq_ref[...]).astype(BF16)
    qi_ref[...] = _dot(xb, wqi_ref[...]).astype(BF16)
    kv = _dot(xb, wkv_ref[...])
    nkv = kv.shape[1] // 2
    k_ref[...] = kv[:, :nkv]
    v_ref[...] = kv[:, nkv:]
    kb_ref[...] = kv[:, :nkv].astype(BF16)
    vb_ref[...] = kv[:, nkv:].astype(BF16)
    kw = _dot(xb, wkw_ref[...])
    ki_ref[...] = kw[:, :IDX_DIM]
    wi_ref[...] = kw[:, IDX_DIM:IDX_DIM + IDX_HEADS]
    lane = lax.broadcasted_iota(I32, kw.shape, 1)
    kilo_ref[...] = jnp.where(lane < IDX_DIM, kw, 0.0).astype(BF16)
    kihi_ref[...] = jnp.where(lane >= IDX_DIM, pltpu.roll(kw, IDX_DIM, 1), 0.0).astype(BF16)


def _attn_proj(x, wq, wqi, wkv, wkw, tm=256):
    n, d = x.shape
    tm = min(tm, n)
    assert n % tm == 0
    row = lambda w: pl.BlockSpec((tm, w), lambda i: (i, 0))
    full = lambda w: pl.BlockSpec(w.shape, lambda i: (0, 0))
    nq, nqi, nkv = wq.shape[1], wqi.shape[1], wkv.shape[1] // 2
    outs = [
        (nq, BF16), (nqi, BF16), (nkv, F32), (nkv, F32), (nkv, BF16), (nkv, BF16),
        (IDX_DIM, F32), (LANES, BF16), (LANES, BF16), (IDX_HEADS, F32),
    ]
    return pl.pallas_call(
        _attn_proj_kernel,
        grid=(n // tm,),
        in_specs=[row(d), full(wq), full(wqi), full(wkv), full(wkw)],
        out_specs=[row(w) for w, _ in outs],
        out_shape=[jax.ShapeDtypeStruct((n, w), t) for w, t in outs],
        compiler_params=_params(("parallel",)),
        name="attn_proj",
    )(x, wq, wqi, wkv, wkw)


def _gla_proj_kernel(x_ref, wgq_ref, wgk_ref, wgv_ref, wgr_ref, wlr_ref, wa2_ref, ba_ref,
                     gq_ref, gk_ref, gv_ref, la_ref, gr_ref):
    xb = x_ref[...].astype(BF16)
    gq_ref[...] = _dot(xb, wgq_ref[...]) * (GLA_DK ** -0.5)
    gk_ref[...] = _dot(xb, wgk_ref[...])
    gv_ref[...] = _dot(xb, wgv_ref[...]).astype(BF16)
    gr_ref[...] = _dot(xb, wgr_ref[...])
    lr = _dot(xb, wlr_ref[...])
    z = _dot(lr.astype(BF16), wa2_ref[...]) + ba_ref[...]
    la_ref[...] = (jnp.minimum(z, 0.0) - jnp.log(1.0 + jnp.exp(-jnp.abs(z)))) * (1.0 / GLA_NORMALIZER)


def _gla_proj(x, wgq, wgk, wgv, wgr, wlr, wa2, ba, tm=256):
    n, d = x.shape
    tm = min(tm, n)
    assert n % tm == 0
    row = lambda w: pl.BlockSpec((tm, w), lambda i: (i, 0))
    full = lambda w: pl.BlockSpec(w.shape, lambda i: (0, 0))
    kw, vw = wgq.shape[1], wgv.shape[1]
    outs = [(kw, F32), (kw, F32), (vw, BF16), (kw, F32), (vw, F32)]
    return pl.pallas_call(
        _gla_proj_kernel,
        grid=(n // tm,),
        in_specs=[row(d), full(wgq), full(wgk), full(wgv), full(wgr), full(wlr), full(wa2), full(ba)],
        out_specs=[row(w) for w, _ in outs],
        out_shape=[jax.ShapeDtypeStruct((n, w), t) for w, t in outs],
        compiler_params=_params(("parallel",)),
        name="gla_proj",
    )(x, wgq, wgk, wgv, wgr, wlr, wa2, ba)


def _sortable_key(a):
    bits = lax.bitcast_convert_type(a + 0.0, I32)
    return bits ^ ((bits >> 31) & 0x7FFFFFFF)


def _dsa_kernel(q_ref, qi_ref, wi_ref, k_ref, v_ref, kilo_ref, kihi_ref, bw_ref, o_ref,
                sc_ref, wb_ref, qip_ref, thr_ref, acc_ref, m_ref, l_ref,
                *, tq, cbs, cbf, qb0, n_sel):
    qb = pl.program_id(1) + qb0
    q0 = qb * LANES
    rows = GROUP * tq

    wi = wi_ref[0]
    for h in range(IDX_HEADS):
        wb_ref[h] = jnp.broadcast_to(wi[:, h:h + 1], (tq, LANES))
    qi = qi_ref[0]
    for p in range(IDX_HEADS // 2):
        qip_ref[p * tq:(p + 1) * tq, :] = qi[:, p * LANES:(p + 1) * LANES]
    row = lax.broadcasted_iota(I32, (tq, LANES), 0)
    lane = lax.broadcasted_iota(I32, (tq, LANES), 1)
    limit = q0 + (row // CHUNK + 1) * CHUNK
    tks = cbs * LANES

    def score_tile(kt, carry):
        start = pl.multiple_of(kt * tks, tks)
        qip = qip_ref[...]
        accs = [jnp.zeros((tq, LANES), F32) for _ in range(cbs)]
        for half, kref in enumerate((kilo_ref, kihi_ref)):
            s = _dot_nt(qip, kref[0, pl.ds(start, tks), :])
            for c in range(cbs):
                a = accs[c]
                for p in range(IDX_HEADS // 2):
                    sp = s[p * tq:(p + 1) * tq, c * LANES:(c + 1) * LANES]
                    a = a + jnp.maximum(sp, 0.0) * wb_ref[2 * p + half]
                accs[c] = a
        for c in range(cbs):
            kpos = (kt * cbs + c) * LANES + lane
            key = jnp.where(kpos < limit, _sortable_key(accs[c]), INT_MIN)
            sc_ref[kt * cbs + c] = key
        return carry

    lax.fori_loop(0, (qb + cbs) // cbs, score_tile, 0)

    n_tiles = qb + 1

    def bisect(i, t):
        cand = t ^ lax.shift_left(jnp.int32(1), 31 - i)
        cand_b = jnp.broadcast_to(cand, (tq, LANES))

        def count(tt, acc):
            return acc + jnp.where(sc_ref[tt] >= cand_b, 1.0, 0.0)

        acc = lax.fori_loop(0, n_tiles, count, jnp.zeros((tq, LANES), F32))
        cnt = jnp.sum(acc, axis=1, keepdims=True)
        return jnp.where(cnt >= n_sel, cand, t)

    t = lax.fori_loop(0, 32, bisect, jnp.full((tq, 1), INT_MIN, I32))
    thr_ref[...] = jnp.broadcast_to(jnp.maximum(t, INT_MIN + 1), (tq, LANES))

    scale = HEAD_DIM ** -0.5
    q = q_ref[0]
    thr = thr_ref[...]
    n_far = jnp.maximum(qb - 1, 0) // cbf
    tkf = cbf * LANES

    def flash_update(logits, vt):
        m_prev = m_ref[...]
        m_new = jnp.maximum(m_prev, jnp.max(logits, axis=1, keepdims=True))
        corr = jnp.exp(m_prev - m_new)
        p = jnp.exp(logits - m_new)
        l_ref[...] = corr * l_ref[...] + jnp.sum(p, axis=1, keepdims=True)
        acc_ref[...] = corr * acc_ref[...] + _dot(p.astype(BF16), vt)
        m_ref[...] = m_new

    for kh in range(N_KV_HEADS):
        qg = jnp.concatenate(
            [q[:, (kh * GROUP + g) * HEAD_DIM:(kh * GROUP + g + 1) * HEAD_DIM] for g in range(GROUP)], axis=0)
        m_ref[...] = jnp.full((rows, 1), NEG_BIG, F32)
        l_ref[...] = jnp.zeros((rows, 1), F32)
        acc_ref[...] = jnp.zeros((rows, HEAD_DIM), F32)
        hs = slice(kh * HEAD_DIM, (kh + 1) * HEAD_DIM)

        def drop(tt):
            return jnp.concatenate([jnp.where(sc_ref[tt] >= thr, 0.0, NEG_BIG)] * GROUP, axis=0)

        def far_tile(kt, carry):
            start = pl.multiple_of(kt * tkf, tkf)
            logits = _dot_nt(qg, k_ref[0, pl.ds(start, tkf), hs]) * scale
            logits = logits + jnp.concatenate([drop(kt * cbf + c) for c in range(cbf)], axis=1)
            flash_update(logits, v_ref[0, pl.ds(start, tkf), hs])
            return carry

        lax.fori_loop(0, n_far, far_tile, 0)

        def tail_tile(tt, carry):
            start = pl.multiple_of(tt * LANES, LANES)
            logits = _dot_nt(qg, k_ref[0, pl.ds(start, LANES), hs]) * scale
            w = jnp.maximum(tt - qb + 2, 0)
            flash_update(logits + bw_ref[w, kh] + drop(tt), v_ref[0, pl.ds(start, LANES), hs])
            return carry

        lax.fori_loop(n_far * cbf, qb + 1, tail_tile, 0)

        o = acc_ref[...] / l_ref[...]
        for g in range(GROUP):
            c0 = (kh * GROUP + g) * HEAD_DIM
            o_ref[0, :, c0:c0 + HEAD_DIM] = o[g * tq:(g + 1) * tq].astype(o_ref.dtype)


def _dsa(q, qi, wi, kb, vb, kilo, kihi, bw, *, tq, cbs, cbf, qb0, n_sel):
    bsz, t, aw = q.shape
    length = kb.shape[1]
    assert t % tq == 0 and length % (LANES * max(cbs, cbf)) == 0
    n_tiles = length // LANES
    rows = GROUP * tq
    qspec = lambda w: pl.BlockSpec((1, tq, w), lambda b, i: (b, i, 0))
    kspec = lambda w: pl.BlockSpec((1, length, w), lambda b, i: (b, 0, 0), pipeline_mode=pl.Buffered(1))
    return pl.pallas_call(
        functools.partial(_dsa_kernel, tq=tq, cbs=cbs, cbf=cbf, qb0=qb0, n_sel=n_sel),
        grid=(bsz, t // tq),
        in_specs=[
            qspec(aw), qspec(qi.shape[2]), qspec(wi.shape[2]),
            kspec(kb.shape[2]), kspec(vb.shape[2]), kspec(LANES), kspec(LANES),
            pl.BlockSpec(bw.shape, lambda b, i: (0, 0, 0, 0)),
        ],
        out_specs=qspec(aw),
        out_shape=jax.ShapeDtypeStruct((bsz, t, aw), BF16),
        scratch_shapes=[
            pltpu.VMEM((n_tiles, tq, LANES), I32),
            pltpu.VMEM((IDX_HEADS, tq, LANES), F32),
            pltpu.VMEM((IDX_HEADS // 2 * tq, LANES), BF16),
            pltpu.VMEM((tq, LANES), I32),
            pltpu.VMEM((rows, HEAD_DIM), F32),
            pltpu.VMEM((rows, 1), F32),
            pltpu.VMEM((rows, 1), F32),
        ],
        compiler_params=_params(("parallel", "arbitrary")),
        name="dsa",
    )(q, qi, wi, kb, vb, kilo, kihi, bw)


def _rel_bucket(rel):
    half = N_BUCKETS // 2
    max_exact = half // 2
    ret = jnp.where(rel > 0, half, 0)
    n = jnp.abs(rel)
    nf = jnp.maximum(n, 1).astype(F32)
    large = max_exact + (jnp.log(nf / max_exact) / math.log(MAX_DISTANCE / max_exact)
                         * (half - max_exact)).astype(I32)
    large = jnp.minimum(large, half - 1)
    return ret + jnp.where(n < max_exact, n, large)


def _bias_window(rel_bias, tq):
    i = jnp.arange(tq, dtype=I32)[:, None]
    c = jnp.arange(2 * LANES, dtype=I32)[None, :]
    rel = c - LANES - i
    far = rel_bias[N_BUCKETS // 2 - 1]
    bias = rel_bias[_rel_bucket(rel)] - far
    bias = jnp.where((rel <= -MAX_DISTANCE)[:, :, None], 0.0, bias)
    bias = bias.transpose(2, 0, 1).reshape(N_KV_HEADS, GROUP * tq, 2 * LANES)
    return jnp.stack([jnp.zeros_like(bias[..., :LANES]), bias[..., :LANES], bias[..., LANES:]]).astype(F32)


def _split3(x):
    hi = x.astype(BF16)
    r = x - hi.astype(F32)
    mid = r.astype(BF16)
    lo = (r - mid.astype(F32)).astype(BF16)
    return hi, mid, lo


def _gla_kernel(gq_ref, gk_ref, gv_ref, la_ref, gr_ref, g_ref, s0_ref, o_ref, sf_ref, st_ref, *, chunk, n_chunks):
    step = pl.program_id(1)

    @pl.when(step == 0)
    def _():
        for h in range(GLA_HEADS):
            st_ref[h] = s0_ref[0, h].astype(F32).T

    ri = lax.broadcasted_iota(I32, (chunk, chunk), 0)
    ci = lax.broadcasted_iota(I32, (chunk, chunk), 1)
    tri = jnp.where(ci <= ri, 1.0, 0.0).astype(BF16)
    rk = lax.broadcasted_iota(I32, (chunk, GLA_DK), 0)
    levels = []
    size = chunk // 2
    while size >= GLA_SUB:
        levels.append(size)
        size //= 2

    def ref_rows(b, idx):
        out = []
        start = 0
        while start < chunk:
            end = start
            while end < chunk and idx[end] == idx[start]:
                end += 1
            out.append(jnp.broadcast_to(b[idx[start]:idx[start] + 1, :], (end - start, b.shape[1])))
            start = end
        return jnp.concatenate(out, axis=0)

    def chunk_step(c, carry):
        rs = pl.ds(pl.multiple_of(c * chunk, chunk), chunk)
        for h in range(GLA_HEADS):
            ks = slice(h * GLA_DK, (h + 1) * GLA_DK)
            vs = slice(h * GLA_DV, (h + 1) * GLA_DV)
            qc = gq_ref[0, rs, ks]
            kc = gk_ref[0, rs, ks]
            vc = gv_ref[0, rs, vs]
            hi, mid, lo = _split3(la_ref[0, rs, ks])
            b = _dot(tri, hi) + _dot(tri, mid) + _dot(tri, lo)
            st_prev = st_ref[h]

            bref = ref_rows(b, [(r // GLA_SUB) * GLA_SUB for r in range(chunk)])
            qd = (qc * jnp.exp(b - bref)).astype(BF16)
            kd = (kc * jnp.exp(bref - b)).astype(BF16)
            keep = (ri // GLA_SUB == ci // GLA_SUB) & (ci <= ri)
            a = jnp.where(keep, _dot_nt(qd, kd), 0.0)
            for size in levels:
                bref = ref_rows(b, [(r // (2 * size)) * 2 * size + size - 1 for r in range(chunk)])
                upper = (rk // size) % 2 == 1
                ql = (qc * jnp.exp(jnp.where(upper, b - bref, 0.0))).astype(BF16)
                kl = (kc * jnp.exp(jnp.where(upper, 0.0, bref - b))).astype(BF16)
                keep = (ri // (2 * size) == ci // (2 * size)) & ((ri // size) % 2 == 1) & ((ci // size) % 2 == 0)
                a = jnp.where(keep, _dot_nt(ql, kl), a)

            o = _dot_nt((qc * jnp.exp(b)).astype(BF16), st_prev.astype(BF16)) + _dot(a.astype(BF16), vc)
            b_last = b[chunk - 1:chunk, :]
            kdec = (kc * jnp.exp(b_last - b)).astype(BF16)
            st_ref[h] = jnp.exp(b_last) * st_prev + lax.dot_general(
                vc, kdec, (((0,), (0,)), ((), ())), preferred_element_type=F32)

            mu = jnp.mean(o, axis=-1, keepdims=True)
            d = o - mu
            var = jnp.mean(d * d, axis=-1, keepdims=True)
            gr = gr_ref[0, rs, vs]
            y = d * lax.rsqrt(var + LN_EPS) * g_ref[h] * (gr * jax.nn.sigmoid(gr))
            o_ref[0, rs, vs] = y.astype(o_ref.dtype)
        return carry

    lax.fori_loop(0, n_chunks, chunk_step, 0)

    @pl.when(step == pl.num_programs(1) - 1)
    def _():
        for h in range(GLA_HEADS):
            sf_ref[0, h] = st_ref[h].T.astype(sf_ref.dtype)


def _gla(gq, gk, gv, la, gr, g, s0, *, chunk, n_chunks):
    bsz, t, kw = gq.shape
    vw = gv.shape[2]
    tb = chunk * n_chunks
    assert t % tb == 0
    tok = lambda w: pl.BlockSpec((1, tb, w), lambda b, i: (b, i, 0))
    sspec = pl.BlockSpec((1,) + s0.shape[1:], lambda b, i: (b, 0, 0, 0))
    return pl.pallas_call(
        functools.partial(_gla_kernel, chunk=chunk, n_chunks=n_chunks),
        grid=(bsz, t // tb),
        in_specs=[tok(kw), tok(kw), tok(vw), tok(kw), tok(vw),
                  pl.BlockSpec(g.shape, lambda b, i: (0, 0, 0)), sspec],
        out_specs=[tok(vw), sspec],
        out_shape=[jax.ShapeDtypeStruct((bsz, t, vw), BF16), jax.ShapeDtypeStruct(s0.shape, s0.dtype)],
        scratch_shapes=[pltpu.VMEM((s0.shape[1], s0.shape[3], s0.shape[2]), F32)],
        compiler_params=_params(("parallel", "arbitrary")),
        name="gla",
    )(gq, gk, gv, la, gr, g, s0)


def _out_ln_kernel(x_ref, a_ref, gl_ref, woa_ref, wog_ref, g_ref, b_ref, o_ref, *, alpha):
    mix = _dot(a_ref[...], woa_ref[...]) + _dot(gl_ref[...], wog_ref[...])
    o_ref[...] = _post_norm(x_ref[...], mix, g_ref[...], b_ref[...], alpha)


def _out_ln(x, attn, gla, woa, wog, g, b, alpha, tm=512):
    n, d = x.shape
    tm = min(tm, n)
    assert n % tm == 0
    row = lambda w: pl.BlockSpec((tm, w), lambda i: (i, 0))
    full = lambda w: pl.BlockSpec(w.shape, lambda i: (0, 0))
    return pl.pallas_call(
        functools.partial(_out_ln_kernel, alpha=alpha),
        grid=(n // tm,),
        in_specs=[row(d), row(attn.shape[1]), row(gla.shape[1]), full(woa), full(wog), full(g), full(b)],
        out_specs=row(d),
        out_shape=jax.ShapeDtypeStruct((n, d), F32),
        compiler_params=_params(("parallel",)),
        name="out_ln",
    )(x, attn, gla, woa, wog, g, b)


def _matmul_kernel(x_ref, w_ref, o_ref):
    o_ref[...] = _dot(x_ref[...].astype(BF16), w_ref[...]).astype(o_ref.dtype)


def _matmul(x, w, tm=256, tn=512):
    n, d = x.shape
    m = w.shape[1]
    tm, tn = min(tm, n), min(tn, m)
    assert n % tm == 0 and m % tn == 0
    return pl.pallas_call(
        _matmul_kernel,
        grid=(n // tm, m // tn),
        in_specs=[pl.BlockSpec((tm, d), lambda i, j: (i, 0)), pl.BlockSpec((d, tn), lambda i, j: (0, j))],
        out_specs=pl.BlockSpec((tm, tn), lambda i, j: (i, j)),
        out_shape=jax.ShapeDtypeStruct((n, m), F32),
        compiler_params=_params(("parallel", "parallel")),
        name="matmul",
    )(x, w)


def _mem_ln_kernel(x_ref, wq_ref, mk_ref, mv_ref, wo_ref, g_ref, b_ref, o_ref, *, alpha):
    x = x_ref[0]
    q = _dot(x.astype(BF16), wq_ref[...])
    scale = MEM_HEAD_DIM ** -0.5
    heads = []
    for h in range(MEM_HEADS):
        hs = slice(h * MEM_HEAD_DIM, (h + 1) * MEM_HEAD_DIM)
        logits = _dot_nt(q[:, hs].astype(BF16), mk_ref[0, :, hs]) * scale
        m = jnp.max(logits, axis=-1, keepdims=True)
        p = jnp.exp(logits - m)
        p = p / jnp.sum(p, axis=-1, keepdims=True)
        heads.append(_dot(p.astype(BF16), mv_ref[0, :, hs]))
    o = jnp.concatenate(heads, axis=1).astype(BF16)
    o_ref[0] = _post_norm(x, _dot(o, wo_ref[...]), g_ref[...], b_ref[...], alpha)


def _mem_ln(x, wq, mk, mv, wo, g, b, alpha, tm=512):
    bsz, t, d = x.shape
    tm = min(tm, t)
    assert t % tm == 0
    full = lambda w: pl.BlockSpec(w.shape, lambda bb, i: (0, 0))
    mem = pl.BlockSpec((1,) + mk.shape[1:], lambda bb, i: (bb, 0, 0))
    tok = pl.BlockSpec((1, tm, d), lambda bb, i: (bb, i, 0))
    return pl.pallas_call(
        functools.partial(_mem_ln_kernel, alpha=alpha),
        grid=(bsz, t // tm),
        in_specs=[tok, full(wq), mem, mem, full(wo), full(g), full(b)],
        out_specs=tok,
        out_shape=jax.ShapeDtypeStruct((bsz, t, d), F32),
        compiler_params=_params(("parallel", "parallel")),
        name="mem_ln",
    )(x, wq, mk, mv, wo, g, b)


def _split_w_in(w_in, w_a2, b_a):
    attn_w = N_HEADS * HEAD_DIM
    kv_w = N_KV_HEADS * HEAD_DIM
    gla_kw = GLA_HEADS * GLA_DK
    gla_vw = GLA_HEADS * GLA_DV
    widths = (attn_w, kv_w, kv_w, IDX_HEADS * IDX_DIM, IDX_DIM, IDX_HEADS, gla_kw, gla_kw, gla_vw, GLA_RANK, gla_vw)
    assert sum(widths) == w_in.shape[1]
    offs = [0]
    for w in widths:
        offs.append(offs[-1] + w)
    col = lambda a, b: w_in[:, offs[a]:offs[b]].astype(BF16)
    wq, wkv, wqi = col(0, 1), col(1, 3), col(3, 4)
    wkw = jnp.pad(col(4, 6), ((0, 0), (0, LANES - IDX_DIM - IDX_HEADS)))
    wgq, wgk, wgv, wgr = col(6, 7), col(7, 8), col(8, 9), col(10, 11)
    wlr = jnp.pad(col(9, 10), ((0, 0), (0, LANES - GLA_RANK)))
    wa2 = jnp.pad(w_a2.astype(BF16), ((0, LANES - GLA_RANK), (0, 0)))
    return (wq, wqi, wkv, wkw), (wgq, wgk, wgv, wgr, wlr, wa2, b_a.reshape(1, -1).astype(F32))


def kernel(x_prompt, x_sample, cache_k, cache_v, cache_idx_k, state_gla, cache_mem_k, cache_mem_v, mem_prompt,
           rel_bias, ln_g, ln_b, ffn1_wg, ffn1_wu, ffn1_wd, w_in, w_a2, b_a, gla_norm_g, w_o, w_mq, w_mk, w_mv,
           w_mo, ffn2_wg, ffn2_wu, ffn2_wd):
    depth = w_in.shape[0]
    bp, tp, d = x_prompt.shape
    bs, ts, _ = x_sample.shape
    past = cache_k.shape[2]
    alpha = (2.0 * depth) ** 0.25
    attn_w = N_HEADS * HEAD_DIM
    kv_w = N_KV_HEADS * HEAD_DIM
    mem_w = MEM_HEADS * MEM_HEAD_DIM
    assert tp % LANES == 0 and ts == CHUNK and past % LANES == 0

    xp = x_prompt.reshape(bp * tp, d)
    xs = x_sample.reshape(bs * ts, d)
    pk, pv, pki, pS, pmk, pmv = [], [], [], [], [], []
    sk, sv, ski, sS = [], [], [], []
    bw_p = _bias_window(rel_bias, LANES)
    bw_s = _bias_window(rel_bias, ts)
    for l in range(depth):
        g = lambda i: ln_g[l, i].reshape(1, d)
        b = lambda i: ln_b[l, i].reshape(1, d)
        bf = lambda w: w[l].astype(BF16)
        f1 = (bf(ffn1_wg), bf(ffn1_wu), bf(ffn1_wd))
        f2 = (bf(ffn2_wg), bf(ffn2_wu), bf(ffn2_wd))
        w_attn, w_gla = _split_w_in(w_in[l], w_a2[l], b_a[l])
        woa, wog = w_o[l, :attn_w].astype(BF16), w_o[l, attn_w:].astype(BF16)
        gn = gla_norm_g[l].reshape(GLA_HEADS, 1, GLA_DV).astype(F32)

        xp = _ffn_ln(xp, *f1, g(0), b(0), alpha)
        xs = _ffn_ln(xs, *f1, g(0), b(0), alpha)

        q, qi, k, v, kb, vb, ki, kilo, kihi, wi = _attn_proj(xp, *w_attn)
        r3 = lambda a, n=bp, t=tp: a.reshape(n, t, a.shape[-1])
        attn = _dsa(r3(q), r3(qi), r3(wi), r3(kb), r3(vb), r3(kilo), r3(kihi), bw_p,
                    tq=LANES, cbs=4, cbf=4, qb0=0, n_sel=min(TOPK_MAX, tp // 4))
        gq, gk, gv, la, gr = _gla_proj(xp, *w_gla)
        s0 = jnp.zeros((bp, GLA_HEADS, GLA_DK, GLA_DV), state_gla.dtype)
        go, s_p = _gla(r3(gq), r3(gk), r3(gv), r3(la), r3(gr), gn, s0, chunk=CHUNK, n_chunks=4)
        xp = _out_ln(xp, attn.reshape(bp * tp, attn_w), go.reshape(bp * tp, -1), woa, wog, g(1), b(1), alpha)
        pk.append(k.reshape(bp, tp, N_KV_HEADS, HEAD_DIM))
        pv.append(v.reshape(bp, tp, N_KV_HEADS, HEAD_DIM))
        pki.append(ki.reshape(bp, tp, IDX_DIM))
        pS.append(s_p)

        q, qi, k, v, kb, vb, ki, kilo, kihi, wi = _attn_proj(xs, *w_attn)
        r3s = lambda a: a.reshape(bs, ts, a.shape[-1])
        pad_t = (-(past + ts)) % LANES
        cat = lambda c, n: jnp.pad(jnp.concatenate([c.astype(BF16), r3s(n)], axis=1), ((0, 0), (0, pad_t), (0, 0)))
        k_all = cat(cache_k[l].reshape(bs, past, kv_w), kb)
        v_all = cat(cache_v[l].reshape(bs, past, kv_w), vb)
        ci = cache_idx_k[l]
        zi = jnp.zeros_like(ci)
        kilo_all = cat(jnp.concatenate([ci, zi], axis=-1), kilo)
        kihi_all = cat(jnp.concatenate([zi, ci], axis=-1), kihi)
        attn = _dsa(r3s(q), r3s(qi), r3s(wi), k_all, v_all, kilo_all, kihi_all, bw_s,
                    tq=ts, cbs=1, cbf=1, qb0=past // LANES, n_sel=min(TOPK_MAX, (past + ts) // 4))
        gq, gk, gv, la, gr = _gla_proj(xs, *w_gla)
        go, s_s = _gla(r3s(gq), r3s(gk), r3s(gv), r3s(la), r3s(gr), gn, state_gla[l], chunk=ts, n_chunks=1)
        xs = _out_ln(xs, attn.reshape(bs * ts, attn_w), go.reshape(bs * ts, -1), woa, wog, g(1), b(1), alpha)
        sk.append(k.reshape(bs, ts, N_KV_HEADS, HEAD_DIM))
        sv.append(v.reshape(bs, ts, N_KV_HEADS, HEAD_DIM))
        ski.append(ki.reshape(bs, ts, IDX_DIM))
        sS.append(s_s)

        n_mem = mem_prompt.shape[1]
        mkv = _matmul(mem_prompt.reshape(bp * n_mem, d), jnp.concatenate([bf(w_mk), bf(w_mv)], axis=1))
        mk_p = mkv[:, :mem_w].reshape(bp, n_mem, mem_w)
        mv_p = mkv[:, mem_w:].reshape(bp, n_mem, mem_w)
        wmq, wmo = bf(w_mq), bf(w_mo)
        xp = _mem_ln(xp.reshape(bp, tp, d), wmq, mk_p.astype(BF16), mv_p.astype(BF16), wmo, g(2), b(2),
                     alpha).reshape(bp * tp, d)
        xs = _mem_ln(xs.reshape(bs, ts, d), wmq, cache_mem_k[l].reshape(bs, n_mem, mem_w).astype(BF16),
                     cache_mem_v[l].reshape(bs, n_mem, mem_w).astype(BF16), wmo, g(2), b(2),
                     alpha).reshape(bs * ts, d)
        pmk.append(mk_p.reshape(bp, n_mem, MEM_HEADS, MEM_HEAD_DIM))
        pmv.append(mv_p.reshape(bp, n_mem, MEM_HEADS, MEM_HEAD_DIM))

        xp = _ffn_ln(xp, *f2, g(3), b(3), alpha)
        xs = _ffn_ln(xs, *f2, g(3), b(3), alpha)

    return (xp.reshape(bp, tp, d), xs.reshape(bs, ts, d),
            jnp.stack(pk), jnp.stack(pv), jnp.stack(pki), jnp.stack(pS), jnp.stack(pmk), jnp.stack(pmv),
            jnp.stack(sk), jnp.stack(sv), jnp.stack(ski), jnp.stack(sS))
```

```python
import functools
import math

import jax
import jax.numpy as jnp
from jax import lax
from jax.experimental import pallas as pl
from jax.experimental.pallas import tpu as pltpu

F32 = jnp.float32
BF16 = jnp.bfloat16
I32 = jnp.int32

LANES = 128
SUBLANES = 8
VMEM_BYTES_V7X = 64 * 1024 * 1024
VMEM_LIMIT = VMEM_BYTES_V7X - 8 * 1024 * 1024

CHUNK = 64
N_HEADS = 8
N_KV_HEADS = 2
HEAD_DIM = 128
GROUP = N_HEADS // N_KV_HEADS
IDX_HEADS = 16
IDX_DIM = 64
TOPK_MAX = 256
GLA_HEADS = 4
GLA_DK = 128
GLA_DV = 256
GLA_RANK = 16
GLA_NORMALIZER = 16.0
MEM_HEADS = 4
MEM_HEAD_DIM = 128
N_BUCKETS = 32
MAX_DISTANCE = 128
LN_EPS = 1e-5
INT_MIN = -2 ** 31
NEG_BIG = -1e30
GLA_SUB = 16
COUNT_UNROLL = 4
DSA_SCORE_TILES = 4
DSA_FAR_TILES = 8
LOG2E = math.log2(math.e)


def _round_up(x, m):
    return (x + m - 1) // m * m


def _dot(a, b):
    return jnp.dot(a, b, preferred_element_type=F32)


def _dot_nt(a, b):
    return lax.dot_general(a, b, (((1,), (1,)), ((), ())), preferred_element_type=F32)


def _params(semantics):
    return pltpu.CompilerParams(dimension_semantics=semantics, vmem_limit_bytes=VMEM_LIMIT)


def _post_norm(x, sub, g, b, alpha):
    h = alpha * x + sub
    mu = jnp.mean(h, axis=-1, keepdims=True)
    d = h - mu
    var = jnp.mean(d * d, axis=-1, keepdims=True)
    return d * lax.rsqrt(var + LN_EPS) * g + b


def _ffn_ln_kernel(x_ref, wg_ref, wu_ref, wd_ref, g_ref, b_ref, o_ref, xb_ref, acc_ref, *, alpha):
    j = pl.program_id(1)

    @pl.when(j == 0)
    def _():
        xb_ref[...] = x_ref[...].astype(BF16)
        acc_ref[...] = jnp.zeros_like(acc_ref)

    xb = xb_ref[...]
    hg = _dot(xb, wg_ref[...])
    hu = _dot(xb, wu_ref[...])
    h = (hg * jax.nn.sigmoid(hg)) * hu
    acc_ref[...] += _dot(h.astype(BF16), wd_ref[...])

    @pl.when(j == pl.num_programs(1) - 1)
    def _():
        o_ref[...] = _post_norm(x_ref[...], 0.5 * acc_ref[...], g_ref[...], b_ref[...], alpha)


def _ffn_ln(x, wg, wu, wd, g, b, alpha, tm=512, tf=512):
    n, d = x.shape
    f = wg.shape[1]
    tm = min(tm, n)
    assert n % tm == 0 and f % tf == 0
    return pl.pallas_call(
        functools.partial(_ffn_ln_kernel, alpha=alpha),
        grid=(n // tm, f // tf),
        in_specs=[
            pl.BlockSpec((tm, d), lambda i, j: (i, 0)),
            pl.BlockSpec((d, tf), lambda i, j: (0, j)),
            pl.BlockSpec((d, tf), lambda i, j: (0, j)),
            pl.BlockSpec((tf, d), lambda i, j: (j, 0)),
            pl.BlockSpec((1, d), lambda i, j: (0, 0)),
            pl.BlockSpec((1, d), lambda i, j: (0, 0)),
        ],
        out_specs=pl.BlockSpec((tm, d), lambda i, j: (i, 0)),
        out_shape=jax.ShapeDtypeStruct((n, d), F32),
        scratch_shapes=[pltpu.VMEM((tm, d), BF16), pltpu.VMEM((tm, d), F32)],
        compiler_params=_params(("parallel", "arbitrary")),
        name="ffn_ln",
    )(x, wg, wu, wd, g, b)


def _attn_proj_kernel(x_ref, wq_ref, wqi_ref, wkv_ref, wkw_ref,
                      q_ref, qi_ref, k_ref, v_ref, kb_ref, vb_ref, ki_ref, kilo_ref, kihi_ref, wi_ref):
    xb = x_ref[...].astype(BF16)
    q_ref[...] = (_dot(xb, wq_ref[...]) * (HEAD_DIM ** -0.5 * LOG2E)).astype(BF16)
    qi_ref[...] = _dot(xb, wqi_ref[...]).astype(BF16)
    kv = _dot(xb, wkv_ref[...])
    nkv = kv.shape[1] // 2
    k_ref[...] = kv[:, :nkv]
    v_ref[...] = kv[:, nkv:]
    kb_ref[...] = kv[:, :nkv].astype(BF16)
    kw = _dot(xb, wkw_ref[...])
    ki_ref[...] = kw[:, :IDX_DIM]
    wi_ref[...] = kw[:, IDX_DIM:IDX_DIM + IDX_HEADS]
    lane = lax.broadcasted_iota(I32, kw.shape, 1)
    one_col = jnp.where(lane == 0, 1.0, 0.0).astype(BF16)
    for kh in range(N_KV_HEADS):
        vb_ref[:, 2 * kh * HEAD_DIM:(2 * kh + 1) * HEAD_DIM] = kv[:, nkv + kh * HEAD_DIM:nkv + (kh + 1) * HEAD_DIM].astype(BF16)
        vb_ref[:, (2 * kh + 1) * HEAD_DIM:(2 * kh + 2) * HEAD_DIM] = one_col
    kilo_ref[...] = jnp.where(lane < IDX_DIM, kw, 0.0).astype(BF16)
    kihi_ref[...] = jnp.where(lane >= IDX_DIM, pltpu.roll(kw, IDX_DIM, 1), 0.0).astype(BF16)


def _attn_proj(x, wq, wqi, wkv, wkw, tm=256):
    n, d = x.shape
    tm = min(tm, n)
    assert n % tm == 0
    row = lambda w: pl.BlockSpec((tm, w), lambda i: (i, 0))
    full = lambda w: pl.BlockSpec(w.shape, lambda i: (0, 0))
    nq, nqi, nkv = wq.shape[1], wqi.shape[1], wkv.shape[1] // 2
    outs = [
        (nq, BF16), (nqi, BF16), (nkv, F32), (nkv, F32), (nkv, BF16), (2 * nkv, BF16),
        (IDX_DIM, F32), (LANES, BF16), (LANES, BF16), (IDX_HEADS, F32),
    ]
    return pl.pallas_call(
        _attn_proj_kernel,
        grid=(n // tm,),
        in_specs=[row(d), full(wq), full(wqi), full(wkv), full(wkw)],
        out_specs=[row(w) for w, _ in outs],
        out_shape=[jax.ShapeDtypeStruct((n, w), t) for w, t in outs],
        compiler_params=_params(("parallel",)),
        name="attn_proj",
    )(x, wq, wqi, wkv, wkw)


def _gla_proj_kernel(x_ref, wgq_ref, wgk_ref, wgv_ref, wgr_ref, wlr_ref, wa2_ref, ba_ref,
                     gq_ref, gk_ref, gv_ref, la_ref, gr_ref):
    xb = x_ref[...].astype(BF16)
    gq_ref[...] = _dot(xb, wgq_ref[...]) * (GLA_DK ** -0.5)
    gk_ref[...] = _dot(xb, wgk_ref[...])
    gv_ref[...] = _dot(xb, wgv_ref[...]).astype(BF16)
    gr_ref[...] = _dot(xb, wgr_ref[...])
    lr = _dot(xb, wlr_ref[...])
    z = _dot(lr.astype(BF16), wa2_ref[...]) + ba_ref[...]
    la_ref[...] = (jnp.minimum(z, 0.0) - jnp.log(1.0 + jnp.exp(-jnp.abs(z)))) * (1.0 / GLA_NORMALIZER)


def _gla_proj(x, wgq, wgk, wgv, wgr, wlr, wa2, ba, tm=256):
    n, d = x.shape
    tm = min(tm, n)
    assert n % tm == 0
    row = lambda w: pl.BlockSpec((tm, w), lambda i: (i, 0))
    full = lambda w: pl.BlockSpec(w.shape, lambda i: (0, 0))
    kw, vw = wgq.shape[1], wgv.shape[1]
    outs = [(kw, F32), (kw, F32), (vw, BF16), (kw, F32), (vw, F32)]
    return pl.pallas_call(
        _gla_proj_kernel,
        grid=(n // tm,),
        in_specs=[row(d), full(wgq), full(wgk), full(wgv), full(wgr), full(wlr), full(wa2), full(ba)],
        out_specs=[row(w) for w, _ in outs],
        out_shape=[jax.ShapeDtypeStruct((n, w), t) for w, t in outs],
        compiler_params=_params(("parallel",)),
        name="gla_proj",
    )(x, wgq, wgk, wgv, wgr, wlr, wa2, ba)


def _sortable_key(a):
    bits = lax.bitcast_convert_type(a + 0.0, I32)
    return bits ^ ((bits >> 31) & 0x7FFFFFFF)


def _dsa_kernel(q_ref, qi_ref, wi_ref, k_ref, v_ref, kilo_ref, kihi_ref, bw_ref, o_ref,
                sc_ref, wb_ref, qip_ref, thr_ref, qg_ref, acc_ref, m_ref,
                *, tq, cbs, cbf, qb0, n_sel):
    qb = pl.program_id(1) + qb0
    q0 = qb * LANES
    rows = GROUP * tq

    wi = wi_ref[0]
    for h in range(IDX_HEADS):
        wb_ref[h] = jnp.broadcast_to(wi[:, h:h + 1], (tq, LANES))
    qi = qi_ref[0]
    for p in range(IDX_HEADS // 2):
        qip_ref[p * tq:(p + 1) * tq, :] = qi[:, p * LANES:(p + 1) * LANES]
    row = lax.broadcasted_iota(I32, (tq, LANES), 0)
    lane = lax.broadcasted_iota(I32, (tq, LANES), 1)
    limit = q0 + (row // CHUNK + 1) * CHUNK
    tks = cbs * LANES

    def score_tile(kt, carry):
        start = pl.multiple_of(kt * tks, tks)
        qip = qip_ref[...]
        accs = [jnp.zeros((tq, LANES), F32) for _ in range(cbs)]
        for half, kref in enumerate((kilo_ref, kihi_ref)):
            s = _dot_nt(qip, kref[0, pl.ds(start, tks), :])
            for c in range(cbs):
                a = accs[c]
                for p in range(IDX_HEADS // 2):
                    sp = s[p * tq:(p + 1) * tq, c * LANES:(c + 1) * LANES]
                    a = a + jnp.maximum(sp, 0.0) * wb_ref[2 * p + half]
                accs[c] = a
        for c in range(cbs):
            kpos = (kt * cbs + c) * LANES + lane
            key = jnp.where(kpos < limit, _sortable_key(accs[c]), INT_MIN)
            sc_ref[kt * cbs + c] = key
        return carry

    n_scored = _round_up(jnp.maximum(qb + 1, cbf), max(cbs, COUNT_UNROLL))
    lax.fori_loop(0, n_scored // cbs, score_tile, 0)

    n_groups = _round_up(qb + 1, COUNT_UNROLL) // COUNT_UNROLL

    def bisect(i, t):
        cand = t ^ lax.shift_left(jnp.int32(1), 31 - i)
        cand_b = jnp.broadcast_to(cand, (tq, LANES))

        def count(gi, acc):
            for u in range(COUNT_UNROLL):
                acc = acc + jnp.where(sc_ref[gi * COUNT_UNROLL + u] >= cand_b, 1.0, 0.0)
            return acc

        acc = lax.fori_loop(0, n_groups, count, jnp.zeros((tq, LANES), F32))
        cnt = jnp.sum(acc, axis=1, keepdims=True)
        return jnp.where(cnt >= n_sel, cand, t)

    t = lax.fori_loop(0, 32, bisect, jnp.full((tq, 1), INT_MIN, I32))
    thr_ref[...] = jnp.broadcast_to(jnp.maximum(t, INT_MIN + 1), (tq, LANES))

    q = q_ref[0]
    thr = thr_ref[...]
    vw = 2 * HEAD_DIM

    def flash_update(kh, logits, vt):
        m_prev = m_ref[kh]
        m_new = jnp.maximum(m_prev, jnp.max(logits, axis=1, keepdims=True))
        p = jnp.exp2(logits - m_new)
        acc_ref[kh] = jnp.exp2(m_prev - m_new) * acc_ref[kh] + _dot(p.astype(BF16), vt)
        m_ref[kh] = m_new

    for kh in range(N_KV_HEADS):
        for g in range(GROUP):
            c0 = (kh * GROUP + g) * HEAD_DIM
            qg_ref[kh, g * tq:(g + 1) * tq, :] = q[:, c0:c0 + HEAD_DIM]
        m_ref[kh] = jnp.full((rows, 1), NEG_BIG, F32)
        acc_ref[kh] = jnp.zeros((rows, vw), F32)

    def drop(tt, extra=0.0):
        return jnp.concatenate([jnp.where(sc_ref[tt] >= thr, 0.0, NEG_BIG) + extra] * GROUP, axis=0)

    def attend(start, width, dropped, bias):
        for kh in range(N_KV_HEADS):
            kt_h = k_ref[0, pl.ds(start, width), kh * HEAD_DIM:(kh + 1) * HEAD_DIM]
            logits = _dot_nt(qg_ref[kh], kt_h) + dropped
            if bias is not None:
                logits = logits + bias[kh]
            flash_update(kh, logits, v_ref[0, pl.ds(start, width), kh * vw:(kh + 1) * vw])

    n_free = jnp.maximum(qb - 1, 0)

    def far_tile(j, carry):
        t_hi = n_free - j * cbf
        t_lo = jnp.maximum(t_hi - cbf, 0)
        parts = [drop(t_lo + c, jnp.where(t_lo + c < t_hi, 0.0, NEG_BIG)) for c in range(cbf)]
        attend(pl.multiple_of(t_lo * LANES, LANES), cbf * LANES, jnp.concatenate(parts, axis=1), None)
        return carry

    lax.fori_loop(0, (n_free + cbf - 1) // cbf, far_tile, 0)

    t0 = jnp.maximum(qb - 1, 0)
    variant = jnp.where(qb == 0, 1, 0)
    attend(pl.multiple_of(t0 * LANES, LANES), 2 * LANES, jnp.concatenate([drop(t0), drop(t0 + 1)], axis=1),
           [bw_ref[variant, kh] for kh in range(N_KV_HEADS)])

    for kh in range(N_KV_HEADS):
        acc = acc_ref[kh]
        o = acc[:, :HEAD_DIM] / acc[:, HEAD_DIM:HEAD_DIM + 1]
        for g in range(GROUP):
            c0 = (kh * GROUP + g) * HEAD_DIM
            o_ref[0, :, c0:c0 + HEAD_DIM] = o[g * tq:(g + 1) * tq].astype(o_ref.dtype)


def _dsa(q, qi, wi, kb, vb, kilo, kihi, bw, *, tq, cbs, cbf, qb0, n_sel):
    bsz, t, aw = q.shape
    length = kb.shape[1]
    assert t % tq == 0 and length % (LANES * max(cbs, COUNT_UNROLL)) == 0 and length >= cbf * LANES
    n_tiles = length // LANES
    rows = GROUP * tq
    qspec = lambda w: pl.BlockSpec((1, tq, w), lambda b, i: (b, i, 0))
    kspec = lambda w: pl.BlockSpec((1, length, w), lambda b, i: (b, 0, 0), pipeline_mode=pl.Buffered(1))
    return pl.pallas_call(
        functools.partial(_dsa_kernel, tq=tq, cbs=cbs, cbf=cbf, qb0=qb0, n_sel=n_sel),
        grid=(bsz, t // tq),
        in_specs=[
            qspec(aw), qspec(qi.shape[2]), qspec(wi.shape[2]),
            kspec(kb.shape[2]), kspec(vb.shape[2]), kspec(LANES), kspec(LANES),
            pl.BlockSpec(bw.shape, lambda b, i: (0, 0, 0, 0)),
        ],
        out_specs=qspec(aw),
        out_shape=jax.ShapeDtypeStruct((bsz, t, aw), BF16),
        scratch_shapes=[
            pltpu.VMEM((n_tiles, tq, LANES), I32),
            pltpu.VMEM((IDX_HEADS, tq, LANES), F32),
            pltpu.VMEM((IDX_HEADS // 2 * tq, LANES), BF16),
            pltpu.VMEM((tq, LANES), I32),
            pltpu.VMEM((N_KV_HEADS, rows, HEAD_DIM), BF16),
            pltpu.VMEM((N_KV_HEADS, rows, 2 * HEAD_DIM), F32),
            pltpu.VMEM((N_KV_HEADS, rows, 1), F32),
        ],
        compiler_params=_params(("parallel", "arbitrary")),
        name="dsa",
    )(q, qi, wi, kb, vb, kilo, kihi, bw)


def _rel_bucket(rel):
    half = N_BUCKETS // 2
    max_exact = half // 2
    ret = jnp.where(rel > 0, half, 0)
    n = jnp.abs(rel)
    nf = jnp.maximum(n, 1).astype(F32)
    large = max_exact + (jnp.log(nf / max_exact) / math.log(MAX_DISTANCE / max_exact)
                         * (half - max_exact)).astype(I32)
    large = jnp.minimum(large, half - 1)
    return ret + jnp.where(n < max_exact, n, large)


def _bias_window(rel_bias, tq):
    i = jnp.arange(tq, dtype=I32)[:, None]
    c = jnp.arange(2 * LANES, dtype=I32)[None, :]
    rel = c - LANES - i
    far = rel_bias[N_BUCKETS // 2 - 1]
    bias = (rel_bias[_rel_bucket(rel)] - far) * LOG2E
    bias = jnp.where((rel <= -MAX_DISTANCE)[:, :, None], 0.0, bias)
    bias = bias.transpose(2, 0, 1).reshape(N_KV_HEADS, GROUP * tq, 2 * LANES)
    first = jnp.concatenate([bias[..., LANES:], jnp.zeros_like(bias[..., LANES:])], axis=-1)
    return jnp.stack([bias, first]).astype(F32)


def _split3(x):
    hi = x.astype(BF16)
    r = x - hi.astype(F32)
    mid = r.astype(BF16)
    lo = (r - mid.astype(F32)).astype(BF16)
    return hi, mid, lo


def _gla_kernel(gq_ref, gk_ref, gv_ref, la_ref, gr_ref, g_ref, s0_ref, o_ref, sf_ref, st_ref, *, chunk, n_chunks):
    step = pl.program_id(1)

    @pl.when(step == 0)
    def _():
        for h in range(GLA_HEADS):
            st_ref[h] = s0_ref[0, h].astype(F32).T

    ri = lax.broadcasted_iota(I32, (chunk, chunk), 0)
    ci = lax.broadcasted_iota(I32, (chunk, chunk), 1)
    tri = jnp.where(ci <= ri, 1.0, 0.0).astype(BF16)
    rk = lax.broadcasted_iota(I32, (chunk, GLA_DK), 0)
    levels = []
    size = chunk // 2
    while size >= GLA_SUB:
        levels.append(size)
        size //= 2

    def ref_rows(b, idx):
        out = []
        start = 0
        while start < chunk:
            end = start
            while end < chunk and idx[end] == idx[start]:
                end += 1
            out.append(jnp.broadcast_to(b[idx[start]:idx[start] + 1, :], (end - start, b.shape[1])))
            start = end
        return jnp.concatenate(out, axis=0)

    def chunk_step(c, carry):
        rs = pl.ds(pl.multiple_of(c * chunk, chunk), chunk)
        for h in range(GLA_HEADS):
            ks = slice(h * GLA_DK, (h + 1) * GLA_DK)
            vs = slice(h * GLA_DV, (h + 1) * GLA_DV)
            qc = gq_ref[0, rs, ks]
            kc = gk_ref[0, rs, ks]
            vc = gv_ref[0, rs, vs]
            hi, mid, lo = _split3(la_ref[0, rs, ks])
            b = _dot(tri, hi) + _dot(tri, mid) + _dot(tri, lo)
            st_prev = st_ref[h]

            bref = ref_rows(b, [(r // GLA_SUB) * GLA_SUB for r in range(chunk)])
            qd = (qc * jnp.exp(b - bref)).astype(BF16)
            kd = (kc * jnp.exp(bref - b)).astype(BF16)
            keep = (ri // GLA_SUB == ci // GLA_SUB) & (ci <= ri)
            a = jnp.where(keep, _dot_nt(qd, kd), 0.0)
            for size in levels:
                bref = ref_rows(b, [(r // (2 * size)) * 2 * size + size - 1 for r in range(chunk)])
                upper = (rk // size) % 2 == 1
                ql = (qc * jnp.exp(jnp.where(upper, b - bref, 0.0))).astype(BF16)
                kl = (kc * jnp.exp(jnp.where(upper, 0.0, bref - b))).astype(BF16)
                keep = (ri // (2 * size) == ci // (2 * size)) & ((ri // size) % 2 == 1) & ((ci // size) % 2 == 0)
                a = jnp.where(keep, _dot_nt(ql, kl), a)

            o = _dot_nt((qc * jnp.exp(b)).astype(BF16), st_prev.astype(BF16)) + _dot(a.astype(BF16), vc)
            b_last = b[chunk - 1:chunk, :]
            kdec = (kc * jnp.exp(b_last - b)).astype(BF16)
            st_ref[h] = jnp.exp(b_last) * st_prev + lax.dot_general(
                vc, kdec, (((0,), (0,)), ((), ())), preferred_element_type=F32)

            mu = jnp.mean(o, axis=-1, keepdims=True)
            d = o - mu
            var = jnp.mean(d * d, axis=-1, keepdims=True)
            gr = gr_ref[0, rs, vs]
            y = d * lax.rsqrt(var + LN_EPS) * g_ref[h] * (gr * jax.nn.sigmoid(gr))
            o_ref[0, rs, vs] = y.astype(o_ref.dtype)
        return carry

    lax.fori_loop(0, n_chunks, chunk_step, 0)

    @pl.when(step == pl.num_programs(1) - 1)
    def _():
        for h in range(GLA_HEADS):
            sf_ref[0, h] = st_ref[h].T.astype(sf_ref.dtype)


def _gla(gq, gk, gv, la, gr, g, s0, *, chunk, n_chunks):
    bsz, t, kw = gq.shape
    vw = gv.shape[2]
    tb = chunk * n_chunks
    assert t % tb == 0
    tok = lambda w: pl.BlockSpec((1, tb, w), lambda b, i: (b, i, 0))
    sspec = pl.BlockSpec((1,) + s0.shape[1:], lambda b, i: (b, 0, 0, 0))
    return pl.pallas_call(
        functools.partial(_gla_kernel, chunk=chunk, n_chunks=n_chunks),
        grid=(bsz, t // tb),
        in_specs=[tok(kw), tok(kw), tok(vw), tok(kw), tok(vw),
                  pl.BlockSpec(g.shape, lambda b, i: (0, 0, 0)), sspec],
        out_specs=[tok(vw), sspec],
        out_shape=[jax.ShapeDtypeStruct((bsz, t, vw), BF16), jax.ShapeDtypeStruct(s0.shape, s0.dtype)],
        scratch_shapes=[pltpu.VMEM((s0.shape[1], s0.shape[3], s0.shape[2]), F32)],
        compiler_params=_params(("parallel", "arbitrary")),
        name="gla",
    )(gq, gk, gv, la, gr, g, s0)


def _out_ln_kernel(x_ref, a_ref, gl_ref, woa_ref, wog_ref, g_ref, b_ref, o_ref, *, alpha):
    mix = _dot(a_ref[...], woa_ref[...]) + _dot(gl_ref[...], wog_ref[...])
    o_ref[...] = _post_norm(x_ref[...], mix, g_ref[...], b_ref[...], alpha)


def _out_ln(x, attn, gla, woa, wog, g, b, alpha, tm=512):
    n, d = x.shape
    tm = min(tm, n)
    assert n % tm == 0
    row = lambda w: pl.BlockSpec((tm, w), lambda i: (i, 0))
    full = lambda w: pl.BlockSpec(w.shape, lambda i: (0, 0))
    return pl.pallas_call(
        functools.partial(_out_ln_kernel, alpha=alpha),
        grid=(n // tm,),
        in_specs=[row(d), row(attn.shape[1]), row(gla.shape[1]), full(woa), full(wog), full(g), full(b)],
        out_specs=row(d),
        out_shape=jax.ShapeDtypeStruct((n, d), F32),
        compiler_params=_params(("parallel",)),
        name="out_ln",
    )(x, attn, gla, woa, wog, g, b)


def _matmul_kernel(x_ref, w_ref, o_ref):
    o_ref[...] = _dot(x_ref[...].astype(BF16), w_ref[...]).astype(o_ref.dtype)


def _matmul(x, w, tm=256, tn=512):
    n, d = x.shape
    m = w.shape[1]
    tm, tn = min(tm, n), min(tn, m)
    assert n % tm == 0 and m % tn == 0
    return pl.pallas_call(
        _matmul_kernel,
        grid=(n // tm, m // tn),
        in_specs=[pl.BlockSpec((tm, d), lambda i, j: (i, 0)), pl.BlockSpec((d, tn), lambda i, j: (0, j))],
        out_specs=pl.BlockSpec((tm, tn), lambda i, j: (i, j)),
        out_shape=jax.ShapeDtypeStruct((n, m), F32),
        compiler_params=_params(("parallel", "parallel")),
        name="matmul",
    )(x, w)


def _mem_ln_kernel(x_ref, wq_ref, mk_ref, mv_ref, wo_ref, g_ref, b_ref, o_ref, *, alpha):
    x = x_ref[0]
    q = _dot(x.astype(BF16), wq_ref[...])
    scale = MEM_HEAD_DIM ** -0.5
    heads = []
    for h in range(MEM_HEADS):
        hs = slice(h * MEM_HEAD_DIM, (h + 1) * MEM_HEAD_DIM)
        logits = _dot_nt(q[:, hs].astype(BF16), mk_ref[0, :, hs]) * scale
        m = jnp.max(logits, axis=-1, keepdims=True)
        p = jnp.exp(logits - m)
        p = p / jnp.sum(p, axis=-1, keepdims=True)
        heads.append(_dot(p.astype(BF16), mv_ref[0, :, hs]))
    o = jnp.concatenate(heads, axis=1).astype(BF16)
    o_ref[0] = _post_norm(x, _dot(o, wo_ref[...]), g_ref[...], b_ref[...], alpha)


def _mem_ln(x, wq, mk, mv, wo, g, b, alpha, tm=512):
    bsz, t, d = x.shape
    tm = min(tm, t)
    assert t % tm == 0
    full = lambda w: pl.BlockSpec(w.shape, lambda bb, i: (0, 0))
    mem = pl.BlockSpec((1,) + mk.shape[1:], lambda bb, i: (bb, 0, 0))
    tok = pl.BlockSpec((1, tm, d), lambda bb, i: (bb, i, 0))
    return pl.pallas_call(
        functools.partial(_mem_ln_kernel, alpha=alpha),
        grid=(bsz, t // tm),
        in_specs=[tok, full(wq), mem, mem, full(wo), full(g), full(b)],
        out_specs=tok,
        out_shape=jax.ShapeDtypeStruct((bsz, t, d), F32),
        compiler_params=_params(("parallel", "parallel")),
        name="mem_ln",
    )(x, wq, mk, mv, wo, g, b)


def _split_w_in(w_in, w_a2, b_a):
    attn_w = N_HEADS * HEAD_DIM
    kv_w = N_KV_HEADS * HEAD_DIM
    gla_kw = GLA_HEADS * GLA_DK
    gla_vw = GLA_HEADS * GLA_DV
    widths = (attn_w, kv_w, kv_w, IDX_HEADS * IDX_DIM, IDX_DIM, IDX_HEADS, gla_kw, gla_kw, gla_vw, GLA_RANK, gla_vw)
    assert sum(widths) == w_in.shape[1]
    offs = [0]
    for w in widths:
        offs.append(offs[-1] + w)
    col = lambda a, b: w_in[:, offs[a]:offs[b]].astype(BF16)
    wq, wkv, wqi = col(0, 1), col(1, 3), col(3, 4)
    wkw = jnp.pad(col(4, 6), ((0, 0), (0, LANES - IDX_DIM - IDX_HEADS)))
    wgq, wgk, wgv, wgr = col(6, 7), col(7, 8), col(8, 9), col(10, 11)
    wlr = jnp.pad(col(9, 10), ((0, 0), (0, LANES - GLA_RANK)))
    wa2 = jnp.pad(w_a2.astype(BF16), ((0, LANES - GLA_RANK), (0, 0)))
    return (wq, wqi, wkv, wkw), (wgq, wgk, wgv, wgr, wlr, wa2, b_a.reshape(1, -1).astype(F32))


def kernel(x_prompt, x_sample, cache_k, cache_v, cache_idx_k, state_gla, cache_mem_k, cache_mem_v, mem_prompt,
           rel_bias, ln_g, ln_b, ffn1_wg, ffn1_wu, ffn1_wd, w_in, w_a2, b_a, gla_norm_g, w_o, w_mq, w_mk, w_mv,
           w_mo, ffn2_wg, ffn2_wu, ffn2_wd):
    depth = w_in.shape[0]
    bp, tp, d = x_prompt.shape
    bs, ts, _ = x_sample.shape
    past = cache_k.shape[2]
    alpha = (2.0 * depth) ** 0.25
    attn_w = N_HEADS * HEAD_DIM
    kv_w = N_KV_HEADS * HEAD_DIM
    mem_w = MEM_HEADS * MEM_HEAD_DIM
    assert tp % LANES == 0 and ts == CHUNK and past % LANES == 0

    xp = x_prompt.reshape(bp * tp, d)
    xs = x_sample.reshape(bs * ts, d)
    pk, pv, pki, pS, pmk, pmv = [], [], [], [], [], []
    sk, sv, ski, sS = [], [], [], []
    bw_p = _bias_window(rel_bias, LANES)
    bw_s = _bias_window(rel_bias, ts)
    for l in range(depth):
        g = lambda i: ln_g[l, i].reshape(1, d)
        b = lambda i: ln_b[l, i].reshape(1, d)
        bf = lambda w: w[l].astype(BF16)
        f1 = (bf(ffn1_wg), bf(ffn1_wu), bf(ffn1_wd))
        f2 = (bf(ffn2_wg), bf(ffn2_wu), bf(ffn2_wd))
        w_attn, w_gla = _split_w_in(w_in[l], w_a2[l], b_a[l])
        woa, wog = w_o[l, :attn_w].astype(BF16), w_o[l, attn_w:].astype(BF16)
        gn = gla_norm_g[l].reshape(GLA_HEADS, 1, GLA_DV).astype(F32)

        xp = _ffn_ln(xp, *f1, g(0), b(0), alpha)
        xs = _ffn_ln(xs, *f1, g(0), b(0), alpha)

        q, qi, k, v, kb, vb, ki, kilo, kihi, wi = _attn_proj(xp, *w_attn)
        r3 = lambda a, n=bp, t=tp: a.reshape(n, t, a.shape[-1])
        attn = _dsa(r3(q), r3(qi), r3(wi), r3(kb), r3(vb), r3(kilo), r3(kihi), bw_p,
                    tq=LANES, cbs=DSA_SCORE_TILES, cbf=DSA_FAR_TILES, qb0=0, n_sel=min(TOPK_MAX, tp // 4))
        gq, gk, gv, la, gr = _gla_proj(xp, *w_gla)
        s0 = jnp.zeros((bp, GLA_HEADS, GLA_DK, GLA_DV), state_gla.dtype)
        go, s_p = _gla(r3(gq), r3(gk), r3(gv), r3(la), r3(gr), gn, s0, chunk=CHUNK, n_chunks=4)
        xp = _out_ln(xp, attn.reshape(bp * tp, attn_w), go.reshape(bp * tp, -1), woa, wog, g(1), b(1), alpha)
        pk.append(k.reshape(bp, tp, N_KV_HEADS, HEAD_DIM))
        pv.append(v.reshape(bp, tp, N_KV_HEADS, HEAD_DIM))
        pki.append(ki.reshape(bp, tp, IDX_DIM))
        pS.append(s_p)

        q, qi, k, v, kb, vb, ki, kilo, kihi, wi = _attn_proj(xs, *w_attn)
        r3s = lambda a: a.reshape(bs, ts, a.shape[-1])
        pad_t = (-(past + ts)) % (LANES * max(DSA_SCORE_TILES, COUNT_UNROLL))
        cat = lambda c, n: jnp.pad(jnp.concatenate([c.astype(BF16), r3s(n)], axis=1), ((0, 0), (0, pad_t), (0, 0)))
        k_all = cat(cache_k[l].reshape(bs, past, kv_w), kb)
        cv = cache_v[l]
        one_col = jnp.zeros_like(cv).at[..., 0].set(1.0)
        v_all = cat(jnp.concatenate([cv, one_col], axis=-1).reshape(bs, past, 2 * kv_w), vb)
        ci = cache_idx_k[l]
        zi = jnp.zeros_like(ci)
        kilo_all = cat(jnp.concatenate([ci, zi], axis=-1), kilo)
        kihi_all = cat(jnp.concatenate([zi, ci], axis=-1), kihi)
        attn = _dsa(r3s(q), r3s(qi), r3s(wi), k_all, v_all, kilo_all, kihi_all, bw_s,
                    tq=ts, cbs=DSA_SCORE_TILES, cbf=DSA_FAR_TILES, qb0=past // LANES,
                    n_sel=min(TOPK_MAX, (past + ts) // 4))
        gq, gk, gv, la, gr = _gla_proj(xs, *w_gla)
        go, s_s = _gla(r3s(gq), r3s(gk), r3s(gv), r3s(la), r3s(gr), gn, state_gla[l], chunk=ts, n_chunks=1)
        xs = _out_ln(xs, attn.reshape(bs * ts, attn_w), go.reshape(bs * ts, -1), woa, wog, g(1), b(1), alpha)
        sk.append(k.reshape(bs, ts, N_KV_HEADS, HEAD_DIM))
        sv.append(v.reshape(bs, ts, N_KV_HEADS, HEAD_DIM))
        ski.append(ki.reshape(bs, ts, IDX_DIM))
        sS.append(s_s)

        n_mem = mem_prompt.shape[1]
        mkv = _matmul(mem_prompt.reshape(bp * n_mem, d), jnp.concatenate([bf(w_mk), bf(w_mv)], axis=1))
        mk_p = mkv[:, :mem_w].reshape(bp, n_mem, mem_w)
        mv_p = mkv[:, mem_w:].reshape(bp, n_mem, mem_w)
        wmq, wmo = bf(w_mq), bf(w_mo)
        xp = _mem_ln(xp.reshape(bp, tp, d), wmq, mk_p.astype(BF16), mv_p.astype(BF16), wmo, g(2), b(2),
                     alpha).reshape(bp * tp, d)
        xs = _mem_ln(xs.reshape(bs, ts, d), wmq, cache_mem_k[l].reshape(bs, n_mem, mem_w).astype(BF16),
                     cache_mem_v[l].reshape(bs, n_mem, mem_w).astype(BF16), wmo, g(2), b(2),
                     alpha).reshape(bs * ts, d)
        pmk.append(mk_p.reshape(bp, n_mem, MEM_HEADS, MEM_HEAD_DIM))
        pmv.append(mv_p.reshape(bp, n_mem, MEM_HEADS, MEM_HEAD_DIM))

        xp = _ffn_ln(xp, *f2, g(3), b(3), alpha)
        xs = _ffn_ln(xs, *f2, g(3), b(3), alpha)

    return (xp.reshape(bp, tp, d), xs.reshape(bs, ts, d),
            jnp.stack(pk), jnp.stack(pv), jnp.stack(pki), jnp.stack(pS), jnp.stack(pmk), jnp.stack(pmv),
            jnp.stack(sk), jnp.stack(sv), jnp.stack(ski), jnp.stack(sS))
```

```python
import functools
import math

import jax
import jax.numpy as jnp
from jax import lax
from jax.experimental import pallas as pl
from jax.experimental.pallas import tpu as pltpu

F32 = jnp.float32
BF16 = jnp.bfloat16
I32 = jnp.int32

LANES = 128
SUBLANES = 8
VMEM_BYTES_V7X = 64 * 1024 * 1024
VMEM_LIMIT = VMEM_BYTES_V7X - 4 * 1024 * 1024

CHUNK = 64
N_HEADS = 8
N_KV_HEADS = 2
HEAD_DIM = 128
GROUP = N_HEADS // N_KV_HEADS
IDX_HEADS = 16
IDX_DIM = 64
TOPK_MAX = 256
GLA_HEADS = 4
GLA_DK = 128
GLA_DV = 256
GLA_RANK = 16
GLA_NORMALIZER = 16.0
MEM_HEADS = 4
MEM_HEAD_DIM = 128
N_BUCKETS = 32
MAX_DISTANCE = 128
LN_EPS = 1e-5
INT_MIN = -2 ** 31
NEG_BIG = -1e30
GLA_SUB = 16
BF16_NAN_CODES = 2 ** 7
COUNT_UNROLL = 4
DSA_SCORE_TILES = 4
DSA_FAR_TILES = 8
LOG2E = math.log2(math.e)


def _round_up(x, m):
    return (x + m - 1) // m * m


def _dot(a, b):
    return jnp.dot(a, b, preferred_element_type=F32)


def _dot_nt(a, b):
    return lax.dot_general(a, b, (((1,), (1,)), ((), ())), preferred_element_type=F32)


def _params(semantics):
    return pltpu.CompilerParams(dimension_semantics=semantics, vmem_limit_bytes=VMEM_LIMIT)


def _post_norm(x, sub, g, b, alpha):
    h = alpha * x + sub
    mu = jnp.mean(h, axis=-1, keepdims=True)
    d = h - mu
    var = jnp.mean(d * d, axis=-1, keepdims=True)
    return d * lax.rsqrt(var + LN_EPS) * g + b


def _ffn_ln_kernel(x_ref, wg_ref, wu_ref, wd_ref, g_ref, b_ref, o_ref, xb_ref, acc_ref, *, alpha):
    j = pl.program_id(1)

    @pl.when(j == 0)
    def _():
        xb_ref[...] = x_ref[...].astype(BF16)
        acc_ref[...] = jnp.zeros_like(acc_ref)

    xb = xb_ref[...]
    hg = _dot(xb, wg_ref[...])
    hu = _dot(xb, wu_ref[...])
    h = (hg * jax.nn.sigmoid(hg)) * hu
    acc_ref[...] += _dot(h.astype(BF16), wd_ref[...])

    @pl.when(j == pl.num_programs(1) - 1)
    def _():
        o_ref[...] = _post_norm(x_ref[...], 0.5 * acc_ref[...], g_ref[...], b_ref[...], alpha)


def _ffn_ln(x, wg, wu, wd, g, b, alpha, tm=512, tf=512):
    n, d = x.shape
    f = wg.shape[1]
    tm = min(tm, n)
    assert n % tm == 0 and f % tf == 0
    return pl.pallas_call(
        functools.partial(_ffn_ln_kernel, alpha=alpha),
        grid=(n // tm, f // tf),
        in_specs=[
            pl.BlockSpec((tm, d), lambda i, j: (i, 0)),
            pl.BlockSpec((d, tf), lambda i, j: (0, j)),
            pl.BlockSpec((d, tf), lambda i, j: (0, j)),
            pl.BlockSpec((tf, d), lambda i, j: (j, 0)),
            pl.BlockSpec((1, d), lambda i, j: (0, 0)),
            pl.BlockSpec((1, d), lambda i, j: (0, 0)),
        ],
        out_specs=pl.BlockSpec((tm, d), lambda i, j: (i, 0)),
        out_shape=jax.ShapeDtypeStruct((n, d), F32),
        scratch_shapes=[pltpu.VMEM((tm, d), BF16), pltpu.VMEM((tm, d), F32)],
        compiler_params=_params(("parallel", "arbitrary")),
        name="ffn_ln",
    )(x, wg, wu, wd, g, b)


def _attn_proj_kernel(x_ref, wq_ref, wqi_ref, wkv_ref, wkw_ref,
                      q_ref, qi_ref, k_ref, v_ref, kb_ref, vb_ref, ki_ref, ki2_ref, wi_ref):
    xb = x_ref[...].astype(BF16)
    q_ref[...] = (_dot(xb, wq_ref[...]) * (HEAD_DIM ** -0.5 * LOG2E)).astype(BF16)
    qi_ref[...] = _dot(xb, wqi_ref[...]).astype(BF16)
    kv = _dot(xb, wkv_ref[...])
    nkv = kv.shape[1] // 2
    k_ref[...] = kv[:, :nkv]
    v_ref[...] = kv[:, nkv:]
    kb_ref[...] = kv[:, :nkv].astype(BF16)
    vb_ref[...] = kv[:, nkv:].astype(BF16)
    kw = _dot(xb, wkw_ref[...])
    ki_ref[...] = kw[:, :IDX_DIM]
    wi_ref[...] = kw[:, IDX_DIM:IDX_DIM + IDX_HEADS]
    lane = lax.broadcasted_iota(I32, kw.shape, 1)
    ki2_ref[...] = jnp.where(lane < IDX_DIM, kw, pltpu.roll(kw, IDX_DIM, 1)).astype(BF16)


def _attn_proj(x, wq, wqi, wkv, wkw, tm=256):
    n, d = x.shape
    tm = min(tm, n)
    assert n % tm == 0
    row = lambda w: pl.BlockSpec((tm, w), lambda i: (i, 0))
    full = lambda w: pl.BlockSpec(w.shape, lambda i: (0, 0))
    nq, nqi, nkv = wq.shape[1], wqi.shape[1], wkv.shape[1] // 2
    outs = [
        (nq, BF16), (nqi, BF16), (nkv, F32), (nkv, F32), (nkv, BF16), (nkv, BF16),
        (IDX_DIM, F32), (LANES, BF16), (IDX_HEADS, F32),
    ]
    return pl.pallas_call(
        _attn_proj_kernel,
        grid=(n // tm,),
        in_specs=[row(d), full(wq), full(wqi), full(wkv), full(wkw)],
        out_specs=[row(w) for w, _ in outs],
        out_shape=[jax.ShapeDtypeStruct((n, w), t) for w, t in outs],
        compiler_params=_params(("parallel",)),
        name="attn_proj",
    )(x, wq, wqi, wkv, wkw)


def _gla_proj_kernel(x_ref, wgq_ref, wgk_ref, wgv_ref, wgr_ref, wlr_ref, wa2_ref, ba_ref,
                     gq_ref, gk_ref, gv_ref, la_ref, gr_ref):
    xb = x_ref[...].astype(BF16)
    gq_ref[...] = _dot(xb, wgq_ref[...]) * (GLA_DK ** -0.5)
    gk_ref[...] = _dot(xb, wgk_ref[...])
    gv_ref[...] = _dot(xb, wgv_ref[...]).astype(BF16)
    gr_ref[...] = _dot(xb, wgr_ref[...])
    lr = _dot(xb, wlr_ref[...])
    z = _dot(lr.astype(BF16), wa2_ref[...]) + ba_ref[...]
    la_ref[...] = (jnp.minimum(z, 0.0) - jnp.log(1.0 + jnp.exp(-jnp.abs(z)))) * (1.0 / GLA_NORMALIZER)


def _gla_proj(x, wgq, wgk, wgv, wgr, wlr, wa2, ba, tm=256):
    n, d = x.shape
    tm = min(tm, n)
    assert n % tm == 0
    row = lambda w: pl.BlockSpec((tm, w), lambda i: (i, 0))
    full = lambda w: pl.BlockSpec(w.shape, lambda i: (0, 0))
    kw, vw = wgq.shape[1], wgv.shape[1]
    outs = [(kw, F32), (kw, F32), (vw, BF16), (kw, F32), (vw, F32)]
    return pl.pallas_call(
        _gla_proj_kernel,
        grid=(n // tm,),
        in_specs=[row(d), full(wgq), full(wgk), full(wgv), full(wgr), full(wlr), full(wa2), full(ba)],
        out_specs=[row(w) for w, _ in outs],
        out_shape=[jax.ShapeDtypeStruct((n, w), t) for w, t in outs],
        compiler_params=_params(("parallel",)),
        name="gla_proj",
    )(x, wgq, wgk, wgv, wgr, wlr, wa2, ba)


def _sortable_key(a):
    bits = lax.bitcast_convert_type(a + 0.0, I32)
    return bits ^ ((bits >> 31) & 0x7FFFFFFF)


def _dsa_kernel(q_ref, qi_ref, wi_ref, k_ref, v_ref, ki2_ref, bw_ref, o_ref,
                sc_ref, sc16_ref, wb_ref, qip_ref, thr_ref, qg_ref, acc_ref, m_ref, lg_ref, corr_ref,
                *, tq, cbs, cbf, qb0, n_sel):
    qb = pl.program_id(1) + qb0
    q0 = qb * LANES
    rows = GROUP * tq

    wi = wi_ref[0]
    for h in range(IDX_HEADS):
        wb_ref[h] = jnp.broadcast_to(wi[:, h:h + 1], (tq, LANES))
    qi = qi_ref[0]
    row = lax.broadcasted_iota(I32, (tq, LANES), 0)
    lane = lax.broadcasted_iota(I32, (tq, LANES), 1)
    for p in range(IDX_HEADS // 2):
        pair = qi[:, p * LANES:(p + 1) * LANES]
        qip_ref[0, p * tq:(p + 1) * tq, :] = jnp.where(lane < IDX_DIM, pair, jnp.zeros_like(pair))
        qip_ref[1, p * tq:(p + 1) * tq, :] = jnp.where(lane >= IDX_DIM, pair, jnp.zeros_like(pair))
    limit = q0 + (row // CHUNK + 1) * CHUNK
    tks = cbs * LANES

    def score_tile(kt, carry):
        start = pl.multiple_of(kt * tks, tks)
        ki2 = ki2_ref[0, pl.ds(start, tks), :]
        accs = [jnp.zeros((tq, LANES), F32) for _ in range(cbs)]
        for half in range(2):
            s = _dot_nt(qip_ref[half], ki2)
            for c in range(cbs):
                a = accs[c]
                for p in range(IDX_HEADS // 2):
                    sp = s[p * tq:(p + 1) * tq, c * LANES:(c + 1) * LANES]
                    a = a + jnp.maximum(sp, 0.0) * wb_ref[2 * p + half]
                accs[c] = a
        for c in range(cbs):
            adm = (kt * cbs + c) * LANES + lane < limit
            bits = lax.bitcast_convert_type(accs[c] + 0.0, I32)
            sc_ref[kt * cbs + c] = jnp.where(adm, bits ^ ((bits >> 31) & 0x7FFFFFFF), INT_MIN)
            top = lax.bitcast_convert_type(bits & -65536, F32)
            sc16_ref[kt * cbs + c] = jnp.where(adm, top, -jnp.inf).astype(BF16)
        return carry

    n_scored = _round_up(jnp.maximum(qb + 1, cbf), max(cbs, COUNT_UNROLL))
    lax.fori_loop(0, n_scored // cbs, score_tile, 0)

    n_groups = _round_up(qb + 1, COUNT_UNROLL) // COUNT_UNROLL
    one16 = jnp.ones((tq, LANES), BF16)
    zero16 = jnp.zeros((tq, LANES), BF16)

    def upper_bit(i, state):
        tu, cnt_t = state
        cand_u = tu | lax.shift_left(jnp.int32(1), 15 - i)
        k32 = (jnp.clip(cand_u, BF16_NAN_CODES, 2 ** 16 - BF16_NAN_CODES) - 2 ** 15) << 16
        cval = lax.bitcast_convert_type(k32 ^ ((k32 >> 31) & 0x7FFFFFFF), F32)
        cand_b = jnp.broadcast_to(cval, (tq, LANES)).astype(BF16)

        def count(gi, acc):
            for u in range(COUNT_UNROLL):
                acc = acc + jnp.where(sc16_ref[gi * COUNT_UNROLL + u] >= cand_b, one16, zero16)
            return acc

        acc = lax.fori_loop(0, n_groups, count, zero16)
        cnt = jnp.sum(acc.astype(F32), axis=1, keepdims=True)
        ok = cnt >= n_sel
        return jnp.where(ok, cand_u, tu), jnp.where(ok, cnt, cnt_t)

    tu, cnt_t = lax.fori_loop(0, 16, upper_bit,
                              (jnp.zeros((tq, 1), I32), jnp.full((tq, 1), float(2 ** 30), F32)))

    def lower_bit(state):
        i, t, cnt_t = state
        cand = t | lax.shift_left(jnp.int32(1), 15 - i)
        cand_b = jnp.broadcast_to(cand, (tq, LANES))

        def count(gi, acc):
            for u in range(COUNT_UNROLL):
                acc = acc + jnp.where(sc_ref[gi * COUNT_UNROLL + u] >= cand_b, 1.0, 0.0)
            return acc

        acc = lax.fori_loop(0, n_groups, count, jnp.zeros((tq, LANES), F32))
        cnt = jnp.sum(acc, axis=1, keepdims=True)
        ok = cnt >= n_sel
        return i + 1, jnp.where(ok, cand, t), jnp.where(ok, cnt, cnt_t)

    def undecided(state):
        i, _, cnt_t = state
        return (i < 16) & (jnp.max(jnp.abs(cnt_t - n_sel)) > 0.0)

    _, t, _ = lax.while_loop(undecided, lower_bit, (jnp.int32(0), (tu - 2 ** 15) << 16, cnt_t))
    thr_ref[...] = jnp.broadcast_to(jnp.maximum(t, INT_MIN + 1), (tq, LANES))

    q = q_ref[0]
    thr = thr_ref[...]
    vw = 2 * HEAD_DIM

    def values(span, kh):
        vt = v_ref[0, span, kh * HEAD_DIM:(kh + 1) * HEAD_DIM]
        one_col = lax.broadcasted_iota(I32, vt.shape, 1) == 0
        return jnp.concatenate([vt, jnp.where(one_col, 1.0, 0.0).astype(BF16)], axis=1)

    def flash_update(kh, logits, vt):
        m_prev = m_ref[kh]
        m_new = jnp.maximum(m_prev, jnp.max(logits, axis=1, keepdims=True))
        p = jnp.exp2(logits - m_new)
        acc_ref[kh] = jnp.exp2(m_prev - m_new) * acc_ref[kh] + _dot(p.astype(BF16), vt)
        m_ref[kh] = m_new

    for kh in range(N_KV_HEADS):
        for g in range(GROUP):
            c0 = (kh * GROUP + g) * HEAD_DIM
            qg_ref[kh, g * tq:(g + 1) * tq, :] = q[:, c0:c0 + HEAD_DIM]
        m_ref[kh] = jnp.full((rows, 1), NEG_BIG, F32)
        acc_ref[kh] = jnp.zeros((rows, vw), F32)

    def drop(tt, extra=0.0):
        return jnp.concatenate([jnp.where(sc_ref[tt] >= thr, 0.0, NEG_BIG) + extra] * GROUP, axis=0)

    def attend(start, width, dropped, bias):
        for kh in range(N_KV_HEADS):
            kt_h = k_ref[0, pl.ds(start, width), kh * HEAD_DIM:(kh + 1) * HEAD_DIM]
            logits = _dot_nt(qg_ref[kh], kt_h) + dropped
            if bias is not None:
                logits = logits + bias[kh]
            flash_update(kh, logits, values(pl.ds(start, width), kh))

    n_free = jnp.maximum(qb - 1, 0)

    n_far = (n_free + cbf - 1) // cbf
    wf = cbf * LANES

    def far_span(j):
        t_hi = n_free - j * cbf
        t_lo = jnp.maximum(t_hi - cbf, 0)
        return t_lo, t_hi, pl.ds(pl.multiple_of(t_lo * LANES, LANES), wf)

    def far_logits(j):
        t_lo, t_hi, span = far_span(j)
        parts = [drop(t_lo + c, jnp.where(t_lo + c < t_hi, 0.0, NEG_BIG)) for c in range(cbf)]
        dropped = jnp.concatenate(parts, axis=1)
        for kh in range(N_KV_HEADS):
            logits = _dot_nt(qg_ref[kh], k_ref[0, span, kh * HEAD_DIM:(kh + 1) * HEAD_DIM]) + dropped
            m_prev = m_ref[kh]
            m_new = jnp.maximum(m_prev, jnp.max(logits, axis=1, keepdims=True))
            lg_ref[j % 2, kh] = logits
            corr_ref[j % 2, kh, 0] = jnp.exp2(m_prev - m_new)
            corr_ref[j % 2, kh, 1] = m_new
            m_ref[kh] = m_new

    def far_values(j):
        _, _, span = far_span(j)
        for kh in range(N_KV_HEADS):
            p = jnp.exp2(lg_ref[j % 2, kh] - corr_ref[j % 2, kh, 1]).astype(BF16)
            acc_ref[kh] = corr_ref[j % 2, kh, 0] * acc_ref[kh] + _dot(p, values(span, kh))

    @pl.when(n_far > 0)
    def _():
        far_logits(0)

    def far_step(j, carry):
        far_values(j)
        far_logits(j + 1)
        return carry

    lax.fori_loop(0, n_far - 1, far_step, 0)

    @pl.when(n_far > 0)
    def _():
        far_values(n_far - 1)

    t0 = jnp.maximum(qb - 1, 0)
    variant = jnp.where(qb == 0, 1, 0)
    attend(pl.multiple_of(t0 * LANES, LANES), 2 * LANES, jnp.concatenate([drop(t0), drop(t0 + 1)], axis=1),
           [bw_ref[variant, kh] for kh in range(N_KV_HEADS)])

    for kh in range(N_KV_HEADS):
        acc = acc_ref[kh]
        o = acc[:, :HEAD_DIM] / acc[:, HEAD_DIM:HEAD_DIM + 1]
        for g in range(GROUP):
            c0 = (kh * GROUP + g) * HEAD_DIM
            o_ref[0, :, c0:c0 + HEAD_DIM] = o[g * tq:(g + 1) * tq].astype(o_ref.dtype)


def _dsa(q, qi, wi, kb, vb, ki2, bw, *, tq, cbs, cbf, qb0, n_sel):
    bsz, t, aw = q.shape
    length = kb.shape[1]
    assert t % tq == 0 and length % (LANES * max(cbs, COUNT_UNROLL)) == 0 and length >= cbf * LANES
    n_tiles = length // LANES
    rows = GROUP * tq
    qspec = lambda w: pl.BlockSpec((1, tq, w), lambda b, i: (b, i, 0))
    kspec = lambda w: pl.BlockSpec((1, length, w), lambda b, i: (b, 0, 0), pipeline_mode=pl.Buffered(1))
    return pl.pallas_call(
        functools.partial(_dsa_kernel, tq=tq, cbs=cbs, cbf=cbf, qb0=qb0, n_sel=n_sel),
        grid=(bsz, t // tq),
        in_specs=[
            qspec(aw), qspec(qi.shape[2]), qspec(wi.shape[2]),
            kspec(kb.shape[2]), kspec(vb.shape[2]), kspec(ki2.shape[2]),
            pl.BlockSpec(bw.shape, lambda b, i: (0, 0, 0, 0)),
        ],
        out_specs=qspec(aw),
        out_shape=jax.ShapeDtypeStruct((bsz, t, aw), BF16),
        scratch_shapes=[
            pltpu.VMEM((n_tiles, tq, LANES), I32),
            pltpu.VMEM((n_tiles, tq, LANES), BF16),
            pltpu.VMEM((IDX_HEADS, tq, LANES), F32),
            pltpu.VMEM((2, IDX_HEADS // 2 * tq, LANES), BF16),
            pltpu.VMEM((tq, LANES), I32),
            pltpu.VMEM((N_KV_HEADS, rows, HEAD_DIM), BF16),
            pltpu.VMEM((N_KV_HEADS, rows, 2 * HEAD_DIM), F32),
            pltpu.VMEM((N_KV_HEADS, rows, 1), F32),
            pltpu.VMEM((2, N_KV_HEADS, rows, cbf * LANES), F32),
            pltpu.VMEM((2, N_KV_HEADS, 2, rows, 1), F32),
        ],
        compiler_params=_params(("parallel", "arbitrary")),
        name="dsa",
    )(q, qi, wi, kb, vb, ki2, bw)


def _rel_bucket(rel):
    half = N_BUCKETS // 2
    max_exact = half // 2
    ret = jnp.where(rel > 0, half, 0)
    n = jnp.abs(rel)
    nf = jnp.maximum(n, 1).astype(F32)
    large = max_exact + (jnp.log(nf / max_exact) / math.log(MAX_DISTANCE / max_exact)
                         * (half - max_exact)).astype(I32)
    large = jnp.minimum(large, half - 1)
    return ret + jnp.where(n < max_exact, n, large)


def _bias_window(rel_bias, tq):
    i = jnp.arange(tq, dtype=I32)[:, None]
    c = jnp.arange(2 * LANES, dtype=I32)[None, :]
    rel = c - LANES - i
    far = rel_bias[N_BUCKETS // 2 - 1]
    bias = (rel_bias[_rel_bucket(rel)] - far) * LOG2E
    bias = jnp.where((rel <= -MAX_DISTANCE)[:, :, None], 0.0, bias)
    bias = bias.transpose(2, 0, 1).reshape(N_KV_HEADS, GROUP * tq, 2 * LANES)
    first = jnp.concatenate([bias[..., LANES:], jnp.zeros_like(bias[..., LANES:])], axis=-1)
    return jnp.stack([bias, first]).astype(F32)


def _split3(x):
    hi = x.astype(BF16)
    r = x - hi.astype(F32)
    mid = r.astype(BF16)
    lo = (r - mid.astype(F32)).astype(BF16)
    return hi, mid, lo


def _gla_kernel(gq_ref, gk_ref, gv_ref, la_ref, gr_ref, g_ref, s0_ref, o_ref, sf_ref, st_ref, *, chunk, n_chunks):
    step = pl.program_id(1)

    @pl.when(step == 0)
    def _():
        for h in range(GLA_HEADS):
            st_ref[h] = s0_ref[0, h].astype(F32).T

    ri = lax.broadcasted_iota(I32, (chunk, chunk), 0)
    ci = lax.broadcasted_iota(I32, (chunk, chunk), 1)
    tri = jnp.where(ci <= ri, 1.0, 0.0).astype(BF16)
    rk = lax.broadcasted_iota(I32, (chunk, GLA_DK), 0)
    levels = []
    size = chunk // 2
    while size >= GLA_SUB:
        levels.append(size)
        size //= 2

    def ref_rows(b, idx):
        out = []
        start = 0
        while start < chunk:
            end = start
            while end < chunk and idx[end] == idx[start]:
                end += 1
            out.append(jnp.broadcast_to(b[idx[start]:idx[start] + 1, :], (end - start, b.shape[1])))
            start = end
        return jnp.concatenate(out, axis=0)

    def chunk_step(c, carry):
        rs = pl.ds(pl.multiple_of(c * chunk, chunk), chunk)
        for h in range(GLA_HEADS):
            ks = slice(h * GLA_DK, (h + 1) * GLA_DK)
            vs = slice(h * GLA_DV, (h + 1) * GLA_DV)
            qc = gq_ref[0, rs, ks]
            kc = gk_ref[0, rs, ks]
            vc = gv_ref[0, rs, vs]
            hi, mid, lo = _split3(la_ref[0, rs, ks])
            b = _dot(tri, hi) + _dot(tri, mid) + _dot(tri, lo)
            st_prev = st_ref[h]

            bref = ref_rows(b, [(r // GLA_SUB) * GLA_SUB for r in range(chunk)])
            qd = (qc * jnp.exp(b - bref)).astype(BF16)
            kd = (kc * jnp.exp(bref - b)).astype(BF16)
            keep = (ri // GLA_SUB == ci // GLA_SUB) & (ci <= ri)
            a = jnp.where(keep, _dot_nt(qd, kd), 0.0)
            for size in levels:
                bref = ref_rows(b, [(r // (2 * size)) * 2 * size + size - 1 for r in range(chunk)])
                upper = (rk // size) % 2 == 1
                ql = (qc * jnp.exp(jnp.where(upper, b - bref, 0.0))).astype(BF16)
                kl = (kc * jnp.exp(jnp.where(upper, 0.0, bref - b))).astype(BF16)
                keep = (ri // (2 * size) == ci // (2 * size)) & ((ri // size) % 2 == 1) & ((ci // size) % 2 == 0)
                a = jnp.where(keep, _dot_nt(ql, kl), a)

            o = _dot_nt((qc * jnp.exp(b)).astype(BF16), st_prev.astype(BF16)) + _dot(a.astype(BF16), vc)
            b_last = b[chunk - 1:chunk, :]
            kdec = (kc * jnp.exp(b_last - b)).astype(BF16)
            st_ref[h] = jnp.exp(b_last) * st_prev + lax.dot_general(
                vc, kdec, (((0,), (0,)), ((), ())), preferred_element_type=F32)

            mu = jnp.mean(o, axis=-1, keepdims=True)
            d = o - mu
            var = jnp.mean(d * d, axis=-1, keepdims=True)
            gr = gr_ref[0, rs, vs]
            y = d * lax.rsqrt(var + LN_EPS) * g_ref[h] * (gr * jax.nn.sigmoid(gr))
            o_ref[0, rs, vs] = y.astype(o_ref.dtype)
        return carry

    lax.fori_loop(0, n_chunks, chunk_step, 0)

    @pl.when(step == pl.num_programs(1) - 1)
    def _():
        for h in range(GLA_HEADS):
            sf_ref[0, h] = st_ref[h].T.astype(sf_ref.dtype)


def _gla(gq, gk, gv, la, gr, g, s0, *, chunk, n_chunks):
    bsz, t, kw = gq.shape
    vw = gv.shape[2]
    tb = chunk * n_chunks
    assert t % tb == 0
    tok = lambda w: pl.BlockSpec((1, tb, w), lambda b, i: (b, i, 0))
    sspec = pl.BlockSpec((1,) + s0.shape[1:], lambda b, i: (b, 0, 0, 0))
    return pl.pallas_call(
        functools.partial(_gla_kernel, chunk=chunk, n_chunks=n_chunks),
        grid=(bsz, t // tb),
        in_specs=[tok(kw), tok(kw), tok(vw), tok(kw), tok(vw),
                  pl.BlockSpec(g.shape, lambda b, i: (0, 0, 0)), sspec],
        out_specs=[tok(vw), sspec],
        out_shape=[jax.ShapeDtypeStruct((bsz, t, vw), BF16), jax.ShapeDtypeStruct(s0.shape, s0.dtype)],
        scratch_shapes=[pltpu.VMEM((s0.shape[1], s0.shape[3], s0.shape[2]), F32)],
        compiler_params=_params(("parallel", "arbitrary")),
        name="gla",
    )(gq, gk, gv, la, gr, g, s0)


def _out_ln_kernel(x_ref, a_ref, gl_ref, woa_ref, wog_ref, g_ref, b_ref, o_ref, *, alpha):
    mix = _dot(a_ref[...], woa_ref[...]) + _dot(gl_ref[...], wog_ref[...])
    o_ref[...] = _post_norm(x_ref[...], mix, g_ref[...], b_ref[...], alpha)


def _out_ln(x, attn, gla, woa, wog, g, b, alpha, tm=512):
    n, d = x.shape
    tm = min(tm, n)
    assert n % tm == 0
    row = lambda w: pl.BlockSpec((tm, w), lambda i: (i, 0))
    full = lambda w: pl.BlockSpec(w.shape, lambda i: (0, 0))
    return pl.pallas_call(
        functools.partial(_out_ln_kernel, alpha=alpha),
        grid=(n // tm,),
        in_specs=[row(d), row(attn.shape[1]), row(gla.shape[1]), full(woa), full(wog), full(g), full(b)],
        out_specs=row(d),
        out_shape=jax.ShapeDtypeStruct((n, d), F32),
        compiler_params=_params(("parallel",)),
        name="out_ln",
    )(x, attn, gla, woa, wog, g, b)


def _matmul_kernel(x_ref, w_ref, o_ref):
    o_ref[...] = _dot(x_ref[...].astype(BF16), w_ref[...]).astype(o_ref.dtype)


def _matmul(x, w, tm=256, tn=512):
    n, d = x.shape
    m = w.shape[1]
    tm, tn = min(tm, n), min(tn, m)
    assert n % tm == 0 and m % tn == 0
    return pl.pallas_call(
        _matmul_kernel,
        grid=(n // tm, m // tn),
        in_specs=[pl.BlockSpec((tm, d), lambda i, j: (i, 0)), pl.BlockSpec((d, tn), lambda i, j: (0, j))],
        out_specs=pl.BlockSpec((tm, tn), lambda i, j: (i, j)),
        out_shape=jax.ShapeDtypeStruct((n, m), F32),
        compiler_params=_params(("parallel", "parallel")),
        name="matmul",
    )(x, w)


def _mem_ln_kernel(x_ref, wq_ref, mk_ref, mv_ref, wo_ref, g_ref, b_ref, o_ref, *, alpha):
    x = x_ref[0]
    q = _dot(x.astype(BF16), wq_ref[...])
    scale = MEM_HEAD_DIM ** -0.5
    heads = []
    for h in range(MEM_HEADS):
        hs = slice(h * MEM_HEAD_DIM, (h + 1) * MEM_HEAD_DIM)
        logits = _dot_nt(q[:, hs].astype(BF16), mk_ref[0, :, hs]) * scale
        m = jnp.max(logits, axis=-1, keepdims=True)
        p = jnp.exp(logits - m)
        p = p / jnp.sum(p, axis=-1, keepdims=True)
        heads.append(_dot(p.astype(BF16), mv_ref[0, :, hs]))
    o = jnp.concatenate(heads, axis=1).astype(BF16)
    o_ref[0] = _post_norm(x, _dot(o, wo_ref[...]), g_ref[...], b_ref[...], alpha)


def _mem_ln(x, wq, mk, mv, wo, g, b, alpha, tm=512):
    bsz, t, d = x.shape
    tm = min(tm, t)
    assert t % tm == 0
    full = lambda w: pl.BlockSpec(w.shape, lambda bb, i: (0, 0))
    mem = pl.BlockSpec((1,) + mk.shape[1:], lambda bb, i: (bb, 0, 0))
    tok = pl.BlockSpec((1, tm, d), lambda bb, i: (bb, i, 0))
    return pl.pallas_call(
        functools.partial(_mem_ln_kernel, alpha=alpha),
        grid=(bsz, t // tm),
        in_specs=[tok, full(wq), mem, mem, full(wo), full(g), full(b)],
        out_specs=tok,
        out_shape=jax.ShapeDtypeStruct((bsz, t, d), F32),
        compiler_params=_params(("parallel", "parallel")),
        name="mem_ln",
    )(x, wq, mk, mv, wo, g, b)


def _split_w_in(w_in, w_a2, b_a):
    attn_w = N_HEADS * HEAD_DIM
    kv_w = N_KV_HEADS * HEAD_DIM
    gla_kw = GLA_HEADS * GLA_DK
    gla_vw = GLA_HEADS * GLA_DV
    widths = (attn_w, kv_w, kv_w, IDX_HEADS * IDX_DIM, IDX_DIM, IDX_HEADS, gla_kw, gla_kw, gla_vw, GLA_RANK, gla_vw)
    assert sum(widths) == w_in.shape[1]
    offs = [0]
    for w in widths:
        offs.append(offs[-1] + w)
    col = lambda a, b: w_in[:, offs[a]:offs[b]].astype(BF16)
    wq, wkv, wqi = col(0, 1), col(1, 3), col(3, 4)
    wkw = jnp.pad(col(4, 6), ((0, 0), (0, LANES - IDX_DIM - IDX_HEADS)))
    wgq, wgk, wgv, wgr = col(6, 7), col(7, 8), col(8, 9), col(10, 11)
    wlr = jnp.pad(col(9, 10), ((0, 0), (0, LANES - GLA_RANK)))
    wa2 = jnp.pad(w_a2.astype(BF16), ((0, LANES - GLA_RANK), (0, 0)))
    return (wq, wqi, wkv, wkw), (wgq, wgk, wgv, wgr, wlr, wa2, b_a.reshape(1, -1).astype(F32))


def kernel(x_prompt, x_sample, cache_k, cache_v, cache_idx_k, state_gla, cache_mem_k, cache_mem_v, mem_prompt,
           rel_bias, ln_g, ln_b, ffn1_wg, ffn1_wu, ffn1_wd, w_in, w_a2, b_a, gla_norm_g, w_o, w_mq, w_mk, w_mv,
           w_mo, ffn2_wg, ffn2_wu, ffn2_wd):
    depth = w_in.shape[0]
    bp, tp, d = x_prompt.shape
    bs, ts, _ = x_sample.shape
    past = cache_k.shape[2]
    alpha = (2.0 * depth) ** 0.25
    attn_w = N_HEADS * HEAD_DIM
    kv_w = N_KV_HEADS * HEAD_DIM
    mem_w = MEM_HEADS * MEM_HEAD_DIM
    assert tp % LANES == 0 and ts == CHUNK and past % LANES == 0

    xp = x_prompt.reshape(bp * tp, d)
    xs = x_sample.reshape(bs * ts, d)
    pk, pv, pki, pS, pmk, pmv = [], [], [], [], [], []
    sk, sv, ski, sS = [], [], [], []
    bw_p = _bias_window(rel_bias, LANES)
    bw_s = _bias_window(rel_bias, ts)
    for l in range(depth):
        g = lambda i: ln_g[l, i].reshape(1, d)
        b = lambda i: ln_b[l, i].reshape(1, d)
        bf = lambda w: w[l].astype(BF16)
        f1 = (bf(ffn1_wg), bf(ffn1_wu), bf(ffn1_wd))
        f2 = (bf(ffn2_wg), bf(ffn2_wu), bf(ffn2_wd))
        w_attn, w_gla = _split_w_in(w_in[l], w_a2[l], b_a[l])
        woa, wog = w_o[l, :attn_w].astype(BF16), w_o[l, attn_w:].astype(BF16)
        gn = gla_norm_g[l].reshape(GLA_HEADS, 1, GLA_DV).astype(F32)

        xp = _ffn_ln(xp, *f1, g(0), b(0), alpha)
        xs = _ffn_ln(xs, *f1, g(0), b(0), alpha)

        q, qi, k, v, kb, vb, ki, ki2, wi = _attn_proj(xp, *w_attn)
        r3 = lambda a, n=bp, t=tp: a.reshape(n, t, a.shape[-1])
        attn = _dsa(r3(q), r3(qi), r3(wi), r3(kb), r3(vb), r3(ki2), bw_p,
                    tq=LANES, cbs=DSA_SCORE_TILES, cbf=DSA_FAR_TILES, qb0=0, n_sel=min(TOPK_MAX, tp // 4))
        gq, gk, gv, la, gr = _gla_proj(xp, *w_gla)
        s0 = jnp.zeros((bp, GLA_HEADS, GLA_DK, GLA_DV), state_gla.dtype)
        go, s_p = _gla(r3(gq), r3(gk), r3(gv), r3(la), r3(gr), gn, s0, chunk=CHUNK, n_chunks=4)
        xp = _out_ln(xp, attn.reshape(bp * tp, attn_w), go.reshape(bp * tp, -1), woa, wog, g(1), b(1), alpha)
        pk.append(k.reshape(bp, tp, N_KV_HEADS, HEAD_DIM))
        pv.append(v.reshape(bp, tp, N_KV_HEADS, HEAD_DIM))
        pki.append(ki.reshape(bp, tp, IDX_DIM))
        pS.append(s_p)

        q, qi, k, v, kb, vb, ki, ki2, wi = _attn_proj(xs, *w_attn)
        r3s = lambda a: a.reshape(bs, ts, a.shape[-1])
        pad_t = (-(past + ts)) % (LANES * max(DSA_SCORE_TILES, COUNT_UNROLL))
        cat = lambda c, n: jnp.pad(jnp.concatenate([c.astype(BF16), r3s(n)], axis=1), ((0, 0), (0, pad_t), (0, 0)))
        k_all = cat(cache_k[l].reshape(bs, past, kv_w), kb)
        v_all = cat(cache_v[l].reshape(bs, past, kv_w), vb)
        ki2_all = cat(jnp.concatenate([cache_idx_k[l]] * 2, axis=-1), ki2)
        attn = _dsa(r3s(q), r3s(qi), r3s(wi), k_all, v_all, ki2_all, bw_s,
                    tq=ts, cbs=DSA_SCORE_TILES, cbf=DSA_FAR_TILES, qb0=past // LANES,
                    n_sel=min(TOPK_MAX, (past + ts) // 4))
        gq, gk, gv, la, gr = _gla_proj(xs, *w_gla)
        go, s_s = _gla(r3s(gq), r3s(gk), r3s(gv), r3s(la), r3s(gr), gn, state_gla[l], chunk=ts, n_chunks=1)
        xs = _out_ln(xs, attn.reshape(bs * ts, attn_w), go.reshape(bs * ts, -1), woa, wog, g(1), b(1), alpha)
        sk.append(k.reshape(bs, ts, N_KV_HEADS, HEAD_DIM))
        sv.append(v.reshape(bs, ts, N_KV_HEADS, HEAD_DIM))
        ski.append(ki.reshape(bs, ts, IDX_DIM))
        sS.append(s_s)

        n_mem = mem_prompt.shape[1]
        mkv = _matmul(mem_prompt.reshape(bp * n_mem, d), jnp.concatenate([bf(w_mk), bf(w_mv)], axis=1))
        mk_p = mkv[:, :mem_w].reshape(bp, n_mem, mem_w)
        mv_p = mkv[:, mem_w:].reshape(bp, n_mem, mem_w)
        wmq, wmo = bf(w_mq), bf(w_mo)
        xp = _mem_ln(xp.reshape(bp, tp, d), wmq, mk_p.astype(BF16), mv_p.astype(BF16), wmo, g(2), b(2),
                     alpha).reshape(bp * tp, d)
        xs = _mem_ln(xs.reshape(bs, ts, d), wmq, cache_mem_k[l].reshape(bs, n_mem, mem_w).astype(BF16),
                     cache_mem_v[l].reshape(bs, n_mem, mem_w).astype(BF16), wmo, g(2), b(2),
                     alpha).reshape(bs * ts, d)
        pmk.append(mk_p.reshape(bp, n_mem, MEM_HEADS, MEM_HEAD_DIM))
        pmv.append(mv_p.reshape(bp, n_mem, MEM_HEADS, MEM_HEAD_DIM))

        xp = _ffn_ln(xp, *f2, g(3), b(3), alpha)
        xs = _ffn_ln(xs, *f2, g(3), b(3), alpha)

    return (xp.reshape(bp, tp, d), xs.reshape(bs, ts, d),
            jnp.stack(pk), jnp.stack(pv), jnp.stack(pki), jnp.stack(pS), jnp.stack(pmk), jnp.stack(pmv),
            jnp.stack(sk), jnp.stack(sv), jnp.stack(ski), jnp.stack(sS))
```

```python
import functools
import math

import jax
import jax.numpy as jnp
from jax import lax
from jax.experimental import pallas as pl
from jax.experimental.pallas import tpu as pltpu

F32 = jnp.float32
BF16 = jnp.bfloat16
I32 = jnp.int32

LANES = 128
SUBLANES = 8
VMEM_BYTES_V7X = 64 * 1024 * 1024
VMEM_LIMIT = VMEM_BYTES_V7X - 4 * 1024 * 1024

CHUNK = 64
N_HEADS = 8
N_KV_HEADS = 2
HEAD_DIM = 128
GROUP = N_HEADS // N_KV_HEADS
IDX_HEADS = 16
IDX_DIM = 64
TOPK_MAX = 256
GLA_HEADS = 4
GLA_DK = 128
GLA_DV = 256
GLA_RANK = 16
GLA_NORMALIZER = 16.0
MEM_HEADS = 4
MEM_HEAD_DIM = 128
N_BUCKETS = 32
MAX_DISTANCE = 128
LN_EPS = 1e-5
INT_MIN = -2 ** 31
NEG_BIG = -1e30
GLA_SUB = 16
GLA_SAFE_EXP = 60.0
BF16_NAN_CODES = 2 ** 7
COUNT_UNROLL = 4
DSA_SCORE_TILES = 8
DSA_FAR_TILES = 8
LOG2E = math.log2(math.e)


def _round_up(x, m):
    return (x + m - 1) // m * m


def _dot(a, b):
    return jnp.dot(a, b, preferred_element_type=F32)


def _dot_nt(a, b):
    return lax.dot_general(a, b, (((1,), (1,)), ((), ())), preferred_element_type=F32)


def _params(semantics):
    return pltpu.CompilerParams(dimension_semantics=semantics, vmem_limit_bytes=VMEM_LIMIT)


def _post_norm(x, sub, g, b, alpha):
    h = alpha * x + sub
    mu = jnp.mean(h, axis=-1, keepdims=True)
    d = h - mu
    var = jnp.mean(d * d, axis=-1, keepdims=True)
    return d * lax.rsqrt(var + LN_EPS) * g + b


def _ffn_ln_kernel(x_ref, wg_ref, wu_ref, wd_ref, g_ref, b_ref, o_ref, xb_ref, acc_ref, *, alpha):
    j = pl.program_id(1)

    @pl.when(j == 0)
    def _():
        xb_ref[...] = x_ref[...].astype(BF16)
        acc_ref[...] = jnp.zeros_like(acc_ref)

    xb = xb_ref[...]
    hg = _dot(xb, wg_ref[...])
    hu = _dot(xb, wu_ref[...])
    h = (hg * jax.nn.sigmoid(hg)) * hu
    acc_ref[...] += _dot(h.astype(BF16), wd_ref[...])

    @pl.when(j == pl.num_programs(1) - 1)
    def _():
        o_ref[...] = _post_norm(x_ref[...], 0.5 * acc_ref[...], g_ref[...], b_ref[...], alpha)


def _ffn_ln(x, wg, wu, wd, g, b, alpha, tm=512, tf=512):
    n, d = x.shape
    f = wg.shape[1]
    tm = min(tm, n)
    assert n % tm == 0 and f % tf == 0
    return pl.pallas_call(
        functools.partial(_ffn_ln_kernel, alpha=alpha),
        grid=(n // tm, f // tf),
        in_specs=[
            pl.BlockSpec((tm, d), lambda i, j: (i, 0)),
            pl.BlockSpec((d, tf), lambda i, j: (0, j)),
            pl.BlockSpec((d, tf), lambda i, j: (0, j)),
            pl.BlockSpec((tf, d), lambda i, j: (j, 0)),
            pl.BlockSpec((1, d), lambda i, j: (0, 0)),
            pl.BlockSpec((1, d), lambda i, j: (0, 0)),
        ],
        out_specs=pl.BlockSpec((tm, d), lambda i, j: (i, 0)),
        out_shape=jax.ShapeDtypeStruct((n, d), F32),
        scratch_shapes=[pltpu.VMEM((tm, d), BF16), pltpu.VMEM((tm, d), F32)],
        compiler_params=_params(("parallel", "arbitrary")),
        name="ffn_ln",
    )(x, wg, wu, wd, g, b)


def _attn_proj_kernel(x_ref, wq_ref, wqi_ref, wkv_ref, wkw_ref,
                      q_ref, qi_ref, k_ref, v_ref, kb_ref, vb_ref, ki_ref, ki2_ref, wi_ref):
    xb = x_ref[...].astype(BF16)
    q_ref[...] = (_dot(xb, wq_ref[...]) * (HEAD_DIM ** -0.5 * LOG2E)).astype(BF16)
    qi_ref[...] = _dot(xb, wqi_ref[...]).astype(BF16)
    kv = _dot(xb, wkv_ref[...])
    nkv = kv.shape[1] // 2
    k_ref[...] = kv[:, :nkv]
    v_ref[...] = kv[:, nkv:]
    kb_ref[...] = kv[:, :nkv].astype(BF16)
    vb_ref[...] = kv[:, nkv:].astype(BF16)
    kw = _dot(xb, wkw_ref[...])
    ki_ref[...] = kw[:, :IDX_DIM]
    wi_ref[...] = kw[:, IDX_DIM:IDX_DIM + IDX_HEADS]
    lane = lax.broadcasted_iota(I32, kw.shape, 1)
    ki2_ref[...] = jnp.where(lane < IDX_DIM, kw, pltpu.roll(kw, IDX_DIM, 1)).astype(BF16)


def _attn_proj(x, wq, wqi, wkv, wkw, tm=256):
    n, d = x.shape
    tm = min(tm, n)
    assert n % tm == 0
    row = lambda w: pl.BlockSpec((tm, w), lambda i: (i, 0))
    full = lambda w: pl.BlockSpec(w.shape, lambda i: (0, 0))
    nq, nqi, nkv = wq.shape[1], wqi.shape[1], wkv.shape[1] // 2
    outs = [
        (nq, BF16), (nqi, BF16), (nkv, F32), (nkv, F32), (nkv, BF16), (nkv, BF16),
        (IDX_DIM, F32), (LANES, BF16), (IDX_HEADS, F32),
    ]
    return pl.pallas_call(
        _attn_proj_kernel,
        grid=(n // tm,),
        in_specs=[row(d), full(wq), full(wqi), full(wkv), full(wkw)],
        out_specs=[row(w) for w, _ in outs],
        out_shape=[jax.ShapeDtypeStruct((n, w), t) for w, t in outs],
        compiler_params=_params(("parallel",)),
        name="attn_proj",
    )(x, wq, wqi, wkv, wkw)


def _gla_proj_kernel(x_ref, wgq_ref, wgk_ref, wgv_ref, wgr_ref, wlr_ref, wa2_ref, ba_ref,
                     gq_ref, gk_ref, gv_ref, la_ref, gr_ref):
    xb = x_ref[...].astype(BF16)
    gq_ref[...] = _dot(xb, wgq_ref[...]) * (GLA_DK ** -0.5)
    gk_ref[...] = _dot(xb, wgk_ref[...])
    gv_ref[...] = _dot(xb, wgv_ref[...]).astype(BF16)
    gr_ref[...] = _dot(xb, wgr_ref[...])
    lr = _dot(xb, wlr_ref[...])
    z = _dot(lr.astype(BF16), wa2_ref[...]) + ba_ref[...]
    la_ref[...] = (jnp.minimum(z, 0.0) - jnp.log(1.0 + jnp.exp(-jnp.abs(z)))) * (1.0 / GLA_NORMALIZER)


def _gla_proj(x, wgq, wgk, wgv, wgr, wlr, wa2, ba, tm=256):
    n, d = x.shape
    tm = min(tm, n)
    assert n % tm == 0
    row = lambda w: pl.BlockSpec((tm, w), lambda i: (i, 0))
    full = lambda w: pl.BlockSpec(w.shape, lambda i: (0, 0))
    kw, vw = wgq.shape[1], wgv.shape[1]
    outs = [(kw, F32), (kw, F32), (vw, BF16), (kw, F32), (vw, F32)]
    return pl.pallas_call(
        _gla_proj_kernel,
        grid=(n // tm,),
        in_specs=[row(d), full(wgq), full(wgk), full(wgv), full(wgr), full(wlr), full(wa2), full(ba)],
        out_specs=[row(w) for w, _ in outs],
        out_shape=[jax.ShapeDtypeStruct((n, w), t) for w, t in outs],
        compiler_params=_params(("parallel",)),
        name="gla_proj",
    )(x, wgq, wgk, wgv, wgr, wlr, wa2, ba)


def _sortable_key(a):
    bits = lax.bitcast_convert_type(a + 0.0, I32)
    return bits ^ ((bits >> 31) & 0x7FFFFFFF)


def _dsa_kernel(q_ref, qi_ref, wi_ref, k_ref, v_ref, ki2_ref, bw_ref, o_ref,
                sc_ref, sc16_ref, wb_ref, qip_ref, thr_ref, qg_ref, acc_ref, m_ref, lg_ref, corr_ref,
                *, tq, cbs, cbf, qb0, n_sel):
    qb = pl.program_id(1) + qb0
    q0 = qb * LANES
    rows = GROUP * tq

    wi = wi_ref[0]
    for h in range(IDX_HEADS):
        wb_ref[h] = jnp.broadcast_to(wi[:, h:h + 1], (tq, LANES))
    qi = qi_ref[0]
    row = lax.broadcasted_iota(I32, (tq, LANES), 0)
    lane = lax.broadcasted_iota(I32, (tq, LANES), 1)
    for p in range(IDX_HEADS // 2):
        pair = qi[:, p * LANES:(p + 1) * LANES]
        qip_ref[0, p * tq:(p + 1) * tq, :] = jnp.where(lane < IDX_DIM, pair, jnp.zeros_like(pair))
        qip_ref[1, p * tq:(p + 1) * tq, :] = jnp.where(lane >= IDX_DIM, pair, jnp.zeros_like(pair))
    limit = q0 + (row // CHUNK + 1) * CHUNK
    tks = cbs * LANES

    def score_tile(kt, carry):
        start = pl.multiple_of(kt * tks, tks)
        ki2 = ki2_ref[0, pl.ds(start, tks), :]
        accs = [jnp.zeros((tq, LANES), F32) for _ in range(cbs)]
        for half in range(2):
            s = _dot_nt(qip_ref[half], ki2)
            for c in range(cbs):
                a = accs[c]
                for p in range(IDX_HEADS // 2):
                    sp = s[p * tq:(p + 1) * tq, c * LANES:(c + 1) * LANES]
                    a = a + jnp.maximum(sp, 0.0) * wb_ref[2 * p + half]
                accs[c] = a
        for c in range(cbs):
            adm = (kt * cbs + c) * LANES + lane < limit
            bits = lax.bitcast_convert_type(accs[c] + 0.0, I32)
            sc_ref[kt * cbs + c] = jnp.where(adm, bits ^ ((bits >> 31) & 0x7FFFFFFF), INT_MIN)
            top = lax.bitcast_convert_type(bits & -65536, F32)
            sc16_ref[kt * cbs + c] = jnp.where(adm, top, -jnp.inf).astype(BF16)
        return carry

    n_scored = _round_up(jnp.maximum(qb + 1, cbf), max(cbs, COUNT_UNROLL))
    lax.fori_loop(0, n_scored // cbs, score_tile, 0)

    n_groups = _round_up(qb + 1, COUNT_UNROLL) // COUNT_UNROLL
    one16 = jnp.ones((tq, LANES), BF16)
    zero16 = jnp.zeros((tq, LANES), BF16)

    def upper_bit(i, state):
        tu, cnt_t = state
        cand_u = tu | lax.shift_left(jnp.int32(1), 15 - i)
        k32 = (jnp.clip(cand_u, BF16_NAN_CODES, 2 ** 16 - BF16_NAN_CODES) - 2 ** 15) << 16
        cval = lax.bitcast_convert_type(k32 ^ ((k32 >> 31) & 0x7FFF0000), F32)
        cand_b = jnp.broadcast_to(cval, (tq, LANES)).astype(BF16)

        def count(gi, acc):
            for u in range(COUNT_UNROLL):
                acc = acc + jnp.where(sc16_ref[gi * COUNT_UNROLL + u] >= cand_b, one16, zero16)
            return acc

        acc = lax.fori_loop(0, n_groups, count, zero16)
        cnt = jnp.sum(acc.astype(F32), axis=1, keepdims=True)
        ok = cnt >= n_sel
        return jnp.where(ok, cand_u, tu), jnp.where(ok, cnt, cnt_t)

    tu, cnt_t = lax.fori_loop(0, 16, upper_bit,
                              (jnp.zeros((tq, 1), I32), jnp.full((tq, 1), float(2 ** 30), F32)))

    def lower_bit(state):
        i, t, cnt_t = state
        cand = t | lax.shift_left(jnp.int32(1), 15 - i)
        cand_b = jnp.broadcast_to(cand, (tq, LANES))

        def count(gi, acc):
            for u in range(COUNT_UNROLL):
                acc = acc + jnp.where(sc_ref[gi * COUNT_UNROLL + u] >= cand_b, 1.0, 0.0)
            return acc

        acc = lax.fori_loop(0, n_groups, count, jnp.zeros((tq, LANES), F32))
        cnt = jnp.sum(acc, axis=1, keepdims=True)
        ok = cnt >= n_sel
        return i + 1, jnp.where(ok, cand, t), jnp.where(ok, cnt, cnt_t)

    def undecided(state):
        i, _, cnt_t = state
        return (i < 16) & (jnp.max(jnp.abs(cnt_t - n_sel)) > 0.0)

    _, t, cnt_t = lax.while_loop(undecided, lower_bit, (jnp.int32(0), (tu - 2 ** 15) << 16, cnt_t))
    thr_ref[...] = jnp.broadcast_to(jnp.maximum(t, INT_MIN + 1), (tq, LANES))

    tied = (cnt_t > n_sel) & (cnt_t < float(2 ** 29))

    @pl.when(jnp.max(jnp.where(tied, 1.0, 0.0)) > 0.0)
    def _():
        t_b = jnp.broadcast_to(t, (tq, LANES))
        tied_b = jnp.broadcast_to(jnp.where(tied, 1, 0), (tq, LANES)) > 0

        def count_where(pred):
            def body(tt, acc):
                return acc + jnp.where(pred(sc_ref[tt], tt * LANES + lane), 1.0, 0.0)
            acc = lax.fori_loop(0, qb + 1, body, jnp.zeros((tq, LANES), F32))
            return jnp.sum(acc, axis=1, keepdims=True)

        keep = n_sel - count_where(lambda key, kpos: key > t_b)
        n_bits = max(1, (sc_ref.shape[0] * LANES - 1).bit_length())

        def position_bit(i, cut):
            cand = cut | lax.shift_left(jnp.int32(1), n_bits - 1 - i)
            cand_b = jnp.broadcast_to(cand, (tq, LANES))
            cnt = count_where(lambda key, kpos: (key == t_b) & (kpos < cand_b))
            return jnp.where(cnt <= keep, cand, cut)

        cut = lax.fori_loop(0, n_bits, position_bit, jnp.zeros((tq, 1), I32))
        cut_b = jnp.broadcast_to(cut, (tq, LANES))

        def demote(tt, carry):
            key = sc_ref[tt]
            late = tied_b & (key == t_b) & (tt * LANES + lane >= cut_b)
            sc_ref[tt] = jnp.where(late, key - 1, key)
            return carry

        lax.fori_loop(0, qb + 1, demote, 0)

    q = q_ref[0]
    thr = thr_ref[...]
    vw = 2 * HEAD_DIM

    def values(span, kh):
        vt = v_ref[0, span, kh * HEAD_DIM:(kh + 1) * HEAD_DIM]
        one_col = lax.broadcasted_iota(I32, vt.shape, 1) == 0
        return jnp.concatenate([vt, jnp.where(one_col, 1.0, 0.0).astype(BF16)], axis=1)

    def flash_update(kh, logits, vt):
        m_prev = m_ref[kh]
        m_new = jnp.maximum(m_prev, jnp.max(logits, axis=1, keepdims=True))
        p = jnp.exp2(logits - m_new)
        acc_ref[kh] = jnp.exp2(m_prev - m_new) * acc_ref[kh] + _dot(p.astype(BF16), vt)
        m_ref[kh] = m_new

    for kh in range(N_KV_HEADS):
        for g in range(GROUP):
            c0 = (kh * GROUP + g) * HEAD_DIM
            qg_ref[kh, g * tq:(g + 1) * tq, :] = q[:, c0:c0 + HEAD_DIM]
        m_ref[kh] = jnp.full((rows, 1), NEG_BIG, F32)
        acc_ref[kh] = jnp.zeros((rows, vw), F32)

    def drop(tt, extra=0.0):
        return jnp.concatenate([jnp.where(sc_ref[tt] >= thr, 0.0, NEG_BIG) + extra] * GROUP, axis=0)

    def attend(start, width, dropped, bias):
        for kh in range(N_KV_HEADS):
            kt_h = k_ref[0, pl.ds(start, width), kh * HEAD_DIM:(kh + 1) * HEAD_DIM]
            logits = _dot_nt(qg_ref[kh], kt_h) + dropped
            if bias is not None:
                logits = logits + bias[kh]
            flash_update(kh, logits, values(pl.ds(start, width), kh))

    n_free = jnp.maximum(qb - 1, 0)

    n_far = (n_free + cbf - 1) // cbf
    wf = cbf * LANES

    def far_span(j):
        t_hi = n_free - j * cbf
        t_lo = jnp.maximum(t_hi - cbf, 0)
        return t_lo, t_hi, pl.ds(pl.multiple_of(t_lo * LANES, LANES), wf)

    def far_logits(j):
        t_lo, t_hi, span = far_span(j)
        parts = [drop(t_lo + c, jnp.where(t_lo + c < t_hi, 0.0, NEG_BIG)) for c in range(cbf)]
        dropped = jnp.concatenate(parts, axis=1)
        for kh in range(N_KV_HEADS):
            logits = _dot_nt(qg_ref[kh], k_ref[0, span, kh * HEAD_DIM:(kh + 1) * HEAD_DIM]) + dropped
            m_prev = m_ref[kh]
            m_new = jnp.maximum(m_prev, jnp.max(logits, axis=1, keepdims=True))
            lg_ref[j % 2, kh] = logits
            corr_ref[j % 2, kh, 0] = jnp.exp2(m_prev - m_new)
            corr_ref[j % 2, kh, 1] = m_new
            m_ref[kh] = m_new

    def far_values(j):
        _, _, span = far_span(j)
        for kh in range(N_KV_HEADS):
            p = jnp.exp2(lg_ref[j % 2, kh] - corr_ref[j % 2, kh, 1]).astype(BF16)
            acc_ref[kh] = corr_ref[j % 2, kh, 0] * acc_ref[kh] + _dot(p, values(span, kh))

    @pl.when(n_far > 0)
    def _():
        far_logits(0)

    def far_step(j, carry):
        far_values(j)
        far_logits(j + 1)
        return carry

    lax.fori_loop(0, n_far - 1, far_step, 0)

    @pl.when(n_far > 0)
    def _():
        far_values(n_far - 1)

    t0 = jnp.maximum(qb - 1, 0)
    variant = jnp.where(qb == 0, 1, 0)
    attend(pl.multiple_of(t0 * LANES, LANES), 2 * LANES, jnp.concatenate([drop(t0), drop(t0 + 1)], axis=1),
           [bw_ref[variant, kh] for kh in range(N_KV_HEADS)])

    for kh in range(N_KV_HEADS):
        acc = acc_ref[kh]
        o = acc[:, :HEAD_DIM] / acc[:, HEAD_DIM:HEAD_DIM + 1]
        for g in range(GROUP):
            c0 = (kh * GROUP + g) * HEAD_DIM
            o_ref[0, :, c0:c0 + HEAD_DIM] = o[g * tq:(g + 1) * tq].astype(o_ref.dtype)


def _dsa(q, qi, wi, kb, vb, ki2, bw, *, tq, cbs, cbf, qb0, n_sel):
    bsz, t, aw = q.shape
    length = kb.shape[1]
    assert t % tq == 0 and length % (LANES * max(cbs, COUNT_UNROLL)) == 0 and length >= cbf * LANES
    n_tiles = length // LANES
    rows = GROUP * tq
    qspec = lambda w: pl.BlockSpec((1, tq, w), lambda b, i: (b, i, 0))
    kspec = lambda w: pl.BlockSpec((1, length, w), lambda b, i: (b, 0, 0), pipeline_mode=pl.Buffered(1))
    return pl.pallas_call(
        functools.partial(_dsa_kernel, tq=tq, cbs=cbs, cbf=cbf, qb0=qb0, n_sel=n_sel),
        grid=(bsz, t // tq),
        in_specs=[
            qspec(aw), qspec(qi.shape[2]), qspec(wi.shape[2]),
            kspec(kb.shape[2]), kspec(vb.shape[2]), kspec(ki2.shape[2]),
            pl.BlockSpec(bw.shape, lambda b, i: (0, 0, 0, 0)),
        ],
        out_specs=qspec(aw),
        out_shape=jax.ShapeDtypeStruct((bsz, t, aw), BF16),
        scratch_shapes=[
            pltpu.VMEM((n_tiles, tq, LANES), I32),
            pltpu.VMEM((n_tiles, tq, LANES), BF16),
            pltpu.VMEM((IDX_HEADS, tq, LANES), F32),
            pltpu.VMEM((2, IDX_HEADS // 2 * tq, LANES), BF16),
            pltpu.VMEM((tq, LANES), I32),
            pltpu.VMEM((N_KV_HEADS, rows, HEAD_DIM), BF16),
            pltpu.VMEM((N_KV_HEADS, rows, 2 * HEAD_DIM), F32),
            pltpu.VMEM((N_KV_HEADS, rows, 1), F32),
            pltpu.VMEM((2, N_KV_HEADS, rows, cbf * LANES), F32),
            pltpu.VMEM((2, N_KV_HEADS, 2, rows, 1), F32),
        ],
        compiler_params=_params(("parallel", "arbitrary")),
        name="dsa",
    )(q, qi, wi, kb, vb, ki2, bw)


def _rel_bucket(rel):
    half = N_BUCKETS // 2
    max_exact = half // 2
    ret = jnp.where(rel > 0, half, 0)
    n = jnp.abs(rel)
    nf = jnp.maximum(n, 1).astype(F32)
    large = max_exact + (jnp.log(nf / max_exact) / math.log(MAX_DISTANCE / max_exact)
                         * (half - max_exact)).astype(I32)
    large = jnp.minimum(large, half - 1)
    return ret + jnp.where(n < max_exact, n, large)


def _bias_window(rel_bias, tq):
    i = jnp.arange(tq, dtype=I32)[:, None]
    c = jnp.arange(2 * LANES, dtype=I32)[None, :]
    rel = c - LANES - i
    far = rel_bias[N_BUCKETS // 2 - 1]
    bias = (rel_bias[_rel_bucket(rel)] - far) * LOG2E
    bias = jnp.where((rel <= -MAX_DISTANCE)[:, :, None], 0.0, bias)
    bias = bias.transpose(2, 0, 1).reshape(N_KV_HEADS, GROUP * tq, 2 * LANES)
    first = jnp.concatenate([bias[..., LANES:], jnp.zeros_like(bias[..., LANES:])], axis=-1)
    return jnp.stack([bias, first]).astype(F32)


def _split3(x):
    hi = x.astype(BF16)
    r = x - hi.astype(F32)
    mid = r.astype(BF16)
    lo = (r - mid.astype(F32)).astype(BF16)
    return hi, mid, lo


def _gla_kernel(gq_ref, gk_ref, gv_ref, la_ref, gr_ref, g_ref, s0_ref, o_ref, sf_ref, st_ref, a_ref, b_ref,
                *, chunk, n_chunks):
    step = pl.program_id(1)

    @pl.when(step == 0)
    def _():
        for h in range(GLA_HEADS):
            st_ref[h] = s0_ref[0, h].astype(F32).T

    ri = lax.broadcasted_iota(I32, (chunk, chunk), 0)
    ci = lax.broadcasted_iota(I32, (chunk, chunk), 1)
    tri = jnp.where(ci <= ri, 1.0, 0.0).astype(BF16)
    rk = lax.broadcasted_iota(I32, (chunk, GLA_DK), 0)
    levels = []
    size = chunk // 2
    while size >= GLA_SUB:
        levels.append(size)
        size //= 2

    def ref_rows(b, idx):
        out = []
        start = 0
        while start < chunk:
            end = start
            while end < chunk and idx[end] == idx[start]:
                end += 1
            out.append(jnp.broadcast_to(b[idx[start]:idx[start] + 1, :], (end - start, b.shape[1])))
            start = end
        return jnp.concatenate(out, axis=0)

    def row_of(rows8, j):
        pick = lax.broadcasted_iota(I32, rows8.shape, 0) == j % SUBLANES
        return jnp.sum(jnp.where(pick, rows8, 0.0), axis=0, keepdims=True)

    def chunk_heads(rs, exact_diagonal):
        for h in range(GLA_HEADS):
            ks = slice(h * GLA_DK, (h + 1) * GLA_DK)
            vs = slice(h * GLA_DV, (h + 1) * GLA_DV)
            qc = gq_ref[0, rs, ks]
            kc = gk_ref[0, rs, ks]
            vc = gv_ref[0, rs, vs]
            hi, mid, lo = _split3(la_ref[0, rs, ks])
            b = _dot(tri, hi) + _dot(tri, mid) + _dot(tri, lo)
            st_prev = st_ref[h]

            keep = (ri // GLA_SUB == ci // GLA_SUB) & (ci <= ri)
            if exact_diagonal:
                a_ref[...] = jnp.zeros_like(a_ref)
                b_ref[...] = b

                def column(j, carry):
                    grp = pl.multiple_of(j // SUBLANES * SUBLANES, SUBLANES)
                    bj = row_of(b_ref[pl.ds(grp, SUBLANES), :], j)
                    kj = row_of(gk_ref[0, pl.ds(rs.start + grp, SUBLANES), ks], j)
                    pair = qc * kj * jnp.exp(jnp.minimum(b - bj, 0.0))
                    col = jnp.sum(pair, axis=1, keepdims=True)
                    a_ref[...] = jnp.where(keep & (ci == j), col, a_ref[...])
                    return carry

                lax.fori_loop(0, chunk, column, 0)
                a = a_ref[...]
            else:
                bref = ref_rows(b, [(r // GLA_SUB) * GLA_SUB for r in range(chunk)])
                qd = (qc * jnp.exp(b - bref)).astype(BF16)
                kd = (kc * jnp.exp(bref - b)).astype(BF16)
                a = jnp.where(keep, _dot_nt(qd, kd), 0.0)
            for size in levels:
                bref = ref_rows(b, [(r // (2 * size)) * 2 * size + size - 1 for r in range(chunk)])
                upper = (rk // size) % 2 == 1
                ql = (qc * jnp.exp(jnp.where(upper, b - bref, 0.0))).astype(BF16)
                kl = (kc * jnp.exp(jnp.where(upper, 0.0, bref - b))).astype(BF16)
                split = (ri // (2 * size) == ci // (2 * size)) & ((ri // size) % 2 == 1) & ((ci // size) % 2 == 0)
                a = jnp.where(split, _dot_nt(ql, kl), a)

            o = _dot_nt((qc * jnp.exp(b)).astype(BF16), st_prev.astype(BF16)) + _dot(a.astype(BF16), vc)
            b_last = b[chunk - 1:chunk, :]
            kdec = (kc * jnp.exp(b_last - b)).astype(BF16)
            st_ref[h] = jnp.exp(b_last) * st_prev + lax.dot_general(
                vc, kdec, (((0,), (0,)), ((), ())), preferred_element_type=F32)

            mu = jnp.mean(o, axis=-1, keepdims=True)
            d = o - mu
            var = jnp.mean(d * d, axis=-1, keepdims=True)
            gr = gr_ref[0, rs, vs]
            y = d * lax.rsqrt(var + LN_EPS) * g_ref[h] * (gr * jax.nn.sigmoid(gr))
            o_ref[0, rs, vs] = y.astype(o_ref.dtype)

    bi = lax.broadcasted_iota(I32, (SUBLANES, chunk), 0)
    rj = lax.broadcasted_iota(I32, (SUBLANES, chunk), 1)
    in_block = jnp.where((rj // GLA_SUB == bi) & (rj % GLA_SUB != 0), 1.0, 0.0).astype(BF16)
    assert chunk // GLA_SUB <= SUBLANES

    def chunk_step(c, carry):
        rs = pl.ds(pl.multiple_of(c * chunk, chunk), chunk)
        steep = jnp.max(_dot(in_block, (-la_ref[0, rs, :]).astype(BF16))) > GLA_SAFE_EXP

        @pl.when(jnp.logical_not(steep))
        def _():
            chunk_heads(rs, False)

        @pl.when(steep)
        def _():
            chunk_heads(rs, True)

        return carry

    lax.fori_loop(0, n_chunks, chunk_step, 0)

    @pl.when(step == pl.num_programs(1) - 1)
    def _():
        for h in range(GLA_HEADS):
            sf_ref[0, h] = st_ref[h].T.astype(sf_ref.dtype)


def _gla(gq, gk, gv, la, gr, g, s0, *, chunk, n_chunks):
    bsz, t, kw = gq.shape
    vw = gv.shape[2]
    tb = chunk * n_chunks
    assert t % tb == 0
    tok = lambda w: pl.BlockSpec((1, tb, w), lambda b, i: (b, i, 0))
    sspec = pl.BlockSpec((1,) + s0.shape[1:], lambda b, i: (b, 0, 0, 0))
    return pl.pallas_call(
        functools.partial(_gla_kernel, chunk=chunk, n_chunks=n_chunks),
        grid=(bsz, t // tb),
        in_specs=[tok(kw), tok(kw), tok(vw), tok(kw), tok(vw),
                  pl.BlockSpec(g.shape, lambda b, i: (0, 0, 0)), sspec],
        out_specs=[tok(vw), sspec],
        out_shape=[jax.ShapeDtypeStruct((bsz, t, vw), BF16), jax.ShapeDtypeStruct(s0.shape, s0.dtype)],
        scratch_shapes=[pltpu.VMEM((s0.shape[1], s0.shape[3], s0.shape[2]), F32),
                        pltpu.VMEM((chunk, chunk), F32), pltpu.VMEM((chunk, GLA_DK), F32)],
        compiler_params=_params(("parallel", "arbitrary")),
        name="gla",
    )(gq, gk, gv, la, gr, g, s0)


def _out_ln_kernel(x_ref, a_ref, gl_ref, woa_ref, wog_ref, g_ref, b_ref, o_ref, *, alpha):
    mix = _dot(a_ref[...], woa_ref[...]) + _dot(gl_ref[...], wog_ref[...])
    o_ref[...] = _post_norm(x_ref[...], mix, g_ref[...], b_ref[...], alpha)


def _out_ln(x, attn, gla, woa, wog, g, b, alpha, tm=512):
    n, d = x.shape
    tm = min(tm, n)
    assert n % tm == 0
    row = lambda w: pl.BlockSpec((tm, w), lambda i: (i, 0))
    full = lambda w: pl.BlockSpec(w.shape, lambda i: (0, 0))
    return pl.pallas_call(
        functools.partial(_out_ln_kernel, alpha=alpha),
        grid=(n // tm,),
        in_specs=[row(d), row(attn.shape[1]), row(gla.shape[1]), full(woa), full(wog), full(g), full(b)],
        out_specs=row(d),
        out_shape=jax.ShapeDtypeStruct((n, d), F32),
        compiler_params=_params(("parallel",)),
        name="out_ln",
    )(x, attn, gla, woa, wog, g, b)


def _matmul_kernel(x_ref, w_ref, o_ref):
    o_ref[...] = _dot(x_ref[...].astype(BF16), w_ref[...]).astype(o_ref.dtype)


def _matmul(x, w, tm=256, tn=512):
    n, d = x.shape
    m = w.shape[1]
    tm, tn = min(tm, n), min(tn, m)
    assert n % tm == 0 and m % tn == 0
    return pl.pallas_call(
        _matmul_kernel,
        grid=(n // tm, m // tn),
        in_specs=[pl.BlockSpec((tm, d), lambda i, j: (i, 0)), pl.BlockSpec((d, tn), lambda i, j: (0, j))],
        out_specs=pl.BlockSpec((tm, tn), lambda i, j: (i, j)),
        out_shape=jax.ShapeDtypeStruct((n, m), F32),
        compiler_params=_params(("parallel", "parallel")),
        name="matmul",
    )(x, w)


def _mem_ln_kernel(x_ref, wq_ref, mk_ref, mv_ref, wo_ref, g_ref, b_ref, o_ref, *, alpha):
    x = x_ref[0]
    q = _dot(x.astype(BF16), wq_ref[...])
    scale = MEM_HEAD_DIM ** -0.5
    heads = []
    for h in range(MEM_HEADS):
        hs = slice(h * MEM_HEAD_DIM, (h + 1) * MEM_HEAD_DIM)
        logits = _dot_nt(q[:, hs].astype(BF16), mk_ref[0, :, hs]) * scale
        m = jnp.max(logits, axis=-1, keepdims=True)
        p = jnp.exp(logits - m)
        p = p / jnp.sum(p, axis=-1, keepdims=True)
        heads.append(_dot(p.astype(BF16), mv_ref[0, :, hs]))
    o = jnp.concatenate(heads, axis=1).astype(BF16)
    o_ref[0] = _post_norm(x, _dot(o, wo_ref[...]), g_ref[...], b_ref[...], alpha)


def _mem_ln(x, wq, mk, mv, wo, g, b, alpha, tm=512):
    bsz, t, d = x.shape
    tm = min(tm, t)
    assert t % tm == 0
    full = lambda w: pl.BlockSpec(w.shape, lambda bb, i: (0, 0))
    mem = pl.BlockSpec((1,) + mk.shape[1:], lambda bb, i: (bb, 0, 0))
    tok = pl.BlockSpec((1, tm, d), lambda bb, i: (bb, i, 0))
    return pl.pallas_call(
        functools.partial(_mem_ln_kernel, alpha=alpha),
        grid=(bsz, t // tm),
        in_specs=[tok, full(wq), mem, mem, full(wo), full(g), full(b)],
        out_specs=tok,
        out_shape=jax.ShapeDtypeStruct((bsz, t, d), F32),
        compiler_params=_params(("parallel", "parallel")),
        name="mem_ln",
    )(x, wq, mk, mv, wo, g, b)


def _split_w_in(w_in, w_a2, b_a):
    attn_w = N_HEADS * HEAD_DIM
    kv_w = N_KV_HEADS * HEAD_DIM
    gla_kw = GLA_HEADS * GLA_DK
    gla_vw = GLA_HEADS * GLA_DV
    widths = (attn_w, kv_w, kv_w, IDX_HEADS * IDX_DIM, IDX_DIM, IDX_HEADS, gla_kw, gla_kw, gla_vw, GLA_RANK, gla_vw)
    assert sum(widths) == w_in.shape[1]
    offs = [0]
    for w in widths:
        offs.append(offs[-1] + w)
    col = lambda a, b: w_in[:, offs[a]:offs[b]].astype(BF16)
    wq, wkv, wqi = col(0, 1), col(1, 3), col(3, 4)
    wkw = jnp.pad(col(4, 6), ((0, 0), (0, LANES - IDX_DIM - IDX_HEADS)))
    wgq, wgk, wgv, wgr = col(6, 7), col(7, 8), col(8, 9), col(10, 11)
    wlr = jnp.pad(col(9, 10), ((0, 0), (0, LANES - GLA_RANK)))
    wa2 = jnp.pad(w_a2.astype(BF16), ((0, LANES - GLA_RANK), (0, 0)))
    return (wq, wqi, wkv, wkw), (wgq, wgk, wgv, wgr, wlr, wa2, b_a.reshape(1, -1).astype(F32))


def kernel(x_prompt, x_sample, cache_k, cache_v, cache_idx_k, state_gla, cache_mem_k, cache_mem_v, mem_prompt,
           rel_bias, ln_g, ln_b, ffn1_wg, ffn1_wu, ffn1_wd, w_in, w_a2, b_a, gla_norm_g, w_o, w_mq, w_mk, w_mv,
           w_mo, ffn2_wg, ffn2_wu, ffn2_wd):
    depth = w_in.shape[0]
    bp, tp, d = x_prompt.shape
    bs, ts, _ = x_sample.shape
    past = cache_k.shape[2]
    alpha = (2.0 * depth) ** 0.25
    attn_w = N_HEADS * HEAD_DIM
    kv_w = N_KV_HEADS * HEAD_DIM
    mem_w = MEM_HEADS * MEM_HEAD_DIM
    assert tp % LANES == 0 and ts == CHUNK and past % LANES == 0

    xp = x_prompt.reshape(bp * tp, d)
    xs = x_sample.reshape(bs * ts, d)
    pk, pv, pki, pS, pmk, pmv = [], [], [], [], [], []
    sk, sv, ski, sS = [], [], [], []
    bw_p = _bias_window(rel_bias, LANES)
    bw_s = _bias_window(rel_bias, ts)
    for l in range(depth):
        g = lambda i: ln_g[l, i].reshape(1, d)
        b = lambda i: ln_b[l, i].reshape(1, d)
        bf = lambda w: w[l].astype(BF16)
        f1 = (bf(ffn1_wg), bf(ffn1_wu), bf(ffn1_wd))
        f2 = (bf(ffn2_wg), bf(ffn2_wu), bf(ffn2_wd))
        w_attn, w_gla = _split_w_in(w_in[l], w_a2[l], b_a[l])
        woa, wog = w_o[l, :attn_w].astype(BF16), w_o[l, attn_w:].astype(BF16)
        gn = gla_norm_g[l].reshape(GLA_HEADS, 1, GLA_DV).astype(F32)

        xp = _ffn_ln(xp, *f1, g(0), b(0), alpha)
        xs = _ffn_ln(xs, *f1, g(0), b(0), alpha)

        q, qi, k, v, kb, vb, ki, ki2, wi = _attn_proj(xp, *w_attn)
        r3 = lambda a, n=bp, t=tp: a.reshape(n, t, a.shape[-1])
        attn = _dsa(r3(q), r3(qi), r3(wi), r3(kb), r3(vb), r3(ki2), bw_p,
                    tq=LANES, cbs=DSA_SCORE_TILES, cbf=DSA_FAR_TILES, qb0=0, n_sel=min(TOPK_MAX, tp // 4))
        gq, gk, gv, la, gr = _gla_proj(xp, *w_gla)
        s0 = jnp.zeros((bp, GLA_HEADS, GLA_DK, GLA_DV), state_gla.dtype)
        go, s_p = _gla(r3(gq), r3(gk), r3(gv), r3(la), r3(gr), gn, s0, chunk=CHUNK, n_chunks=4)
        xp = _out_ln(xp, attn.reshape(bp * tp, attn_w), go.reshape(bp * tp, -1), woa, wog, g(1), b(1), alpha)
        pk.append(k.reshape(bp, tp, N_KV_HEADS, HEAD_DIM))
        pv.append(v.reshape(bp, tp, N_KV_HEADS, HEAD_DIM))
        pki.append(ki.reshape(bp, tp, IDX_DIM))
        pS.append(s_p)

        q, qi, k, v, kb, vb, ki, ki2, wi = _attn_proj(xs, *w_attn)
        r3s = lambda a: a.reshape(bs, ts, a.shape[-1])
        pad_t = (-(past + ts)) % (LANES * max(DSA_SCORE_TILES, COUNT_UNROLL))
        cat = lambda c, n: jnp.pad(jnp.concatenate([c.astype(BF16), r3s(n)], axis=1), ((0, 0), (0, pad_t), (0, 0)))
        k_all = cat(cache_k[l].reshape(bs, past, kv_w), kb)
        v_all = cat(cache_v[l].reshape(bs, past, kv_w), vb)
        ki2_all = cat(jnp.concatenate([cache_idx_k[l]] * 2, axis=-1), ki2)
        attn = _dsa(r3s(q), r3s(qi), r3s(wi), k_all, v_all, ki2_all, bw_s,
                    tq=ts, cbs=DSA_SCORE_TILES, cbf=DSA_FAR_TILES, qb0=past // LANES,
                    n_sel=min(TOPK_MAX, (past + ts) // 4))
        gq, gk, gv, la, gr = _gla_proj(xs, *w_gla)
        go, s_s = _gla(r3s(gq), r3s(gk), r3s(gv), r3s(la), r3s(gr), gn, state_gla[l], chunk=ts, n_chunks=1)
        xs = _out_ln(xs, attn.reshape(bs * ts, attn_w), go.reshape(bs * ts, -1), woa, wog, g(1), b(1), alpha)
        sk.append(k.reshape(bs, ts, N_KV_HEADS, HEAD_DIM))
        sv.append(v.reshape(bs, ts, N_KV_HEADS, HEAD_DIM))
        ski.append(ki.reshape(bs, ts, IDX_DIM))
        sS.append(s_s)

        n_mem = mem_prompt.shape[1]
        mkv = _matmul(mem_prompt.reshape(bp * n_mem, d), jnp.concatenate([bf(w_mk), bf(w_mv)], axis=1))
        mk_p = mkv[:, :mem_w].reshape(bp, n_mem, mem_w)
        mv_p = mkv[:, mem_w:].reshape(bp, n_mem, mem_w)
        wmq, wmo = bf(w_mq), bf(w_mo)
        xp = _mem_ln(xp.reshape(bp, tp, d), wmq, mk_p.astype(BF16), mv_p.astype(BF16), wmo, g(2), b(2),
                     alpha).reshape(bp * tp, d)
        xs = _mem_ln(xs.reshape(bs, ts, d), wmq, cache_mem_k[l].reshape(bs, n_mem, mem_w).astype(BF16),
                     cache_mem_v[l].reshape(bs, n_mem, mem_w).astype(BF16), wmo, g(2), b(2),
                     alpha).reshape(bs * ts, d)
        pmk.append(mk_p.reshape(bp, n_mem, MEM_HEADS, MEM_HEAD_DIM))
        pmv.append(mv_p.reshape(bp, n_mem, MEM_HEADS, MEM_HEAD_DIM))

        xp = _ffn_ln(xp, *f2, g(3), b(3), alpha)
        xs = _ffn_ln(xs, *f2, g(3), b(3), alpha)

    return (xp.reshape(bp, tp, d), xs.reshape(bs, ts, d),
            jnp.stack(pk), jnp.stack(pv), jnp.stack(pki), jnp.stack(pS), jnp.stack(pmk), jnp.stack(pmv),
            jnp.stack(sk), jnp.stack(sv), jnp.stack(ski), jnp.stack(sS))
```

```python
import functools
import math

import jax
import jax.numpy as jnp
from jax import lax
from jax.experimental import pallas as pl
from jax.experimental.pallas import tpu as pltpu

F32 = jnp.float32
BF16 = jnp.bfloat16
I32 = jnp.int32

LANES = 128
SUBLANES = 8
VMEM_BYTES_V7X = 64 * 1024 * 1024
VMEM_LIMIT = VMEM_BYTES_V7X - 4 * 1024 * 1024

CHUNK = 64
N_HEADS = 8
N_KV_HEADS = 2
HEAD_DIM = 128
GROUP = N_HEADS // N_KV_HEADS
IDX_HEADS = 16
IDX_DIM = 64
TOPK_MAX = 256
GLA_HEADS = 4
GLA_DK = 128
GLA_DV = 256
GLA_RANK = 16
GLA_NORMALIZER = 16.0
MEM_HEADS = 4
MEM_HEAD_DIM = 128
N_BUCKETS = 32
MAX_DISTANCE = 128
LN_EPS = 1e-5
INT_MIN = -2 ** 31
NEG_BIG = -1e30
GLA_SUB = 16
GLA_SAFE_EXP = 60.0
BF16_NAN_CODES = 2 ** 7
COUNT_UNROLL = 4
DSA_SCORE_TILES = 8
DSA_FAR_TILES = 8
LOG2E = math.log2(math.e)


def _round_up(x, m):
    return (x + m - 1) // m * m


def _dot(a, b):
    return jnp.dot(a, b, preferred_element_type=F32)


def _dot_nt(a, b):
    return lax.dot_general(a, b, (((1,), (1,)), ((), ())), preferred_element_type=F32)


def _params(semantics):
    return pltpu.CompilerParams(dimension_semantics=semantics, vmem_limit_bytes=VMEM_LIMIT)


def _post_norm(x, sub, g, b, alpha):
    h = alpha * x + sub
    mu = jnp.mean(h, axis=-1, keepdims=True)
    d = h - mu
    var = jnp.mean(d * d, axis=-1, keepdims=True)
    return d * lax.rsqrt(var + LN_EPS) * g + b


def _ffn_ln_kernel(x_ref, wg_ref, wu_ref, wd_ref, g_ref, b_ref, o_ref, xb_ref, acc_ref, *, alpha):
    j = pl.program_id(1)

    @pl.when(j == 0)
    def _():
        xb_ref[...] = x_ref[...].astype(BF16)
        acc_ref[...] = jnp.zeros_like(acc_ref)

    xb = xb_ref[...]
    hg = _dot(xb, wg_ref[...])
    hu = _dot(xb, wu_ref[...])
    h = (hg * jax.nn.sigmoid(hg)) * hu
    acc_ref[...] += _dot(h.astype(BF16), wd_ref[...])

    @pl.when(j == pl.num_programs(1) - 1)
    def _():
        o_ref[...] = _post_norm(x_ref[...], 0.5 * acc_ref[...], g_ref[...], b_ref[...], alpha)


def _ffn_ln(x, wg, wu, wd, g, b, alpha, tm=512, tf=512):
    n, d = x.shape
    f = wg.shape[1]
    tm = min(tm, n)
    assert n % tm == 0 and f % tf == 0
    return pl.pallas_call(
        functools.partial(_ffn_ln_kernel, alpha=alpha),
        grid=(n // tm, f // tf),
        in_specs=[
            pl.BlockSpec((tm, d), lambda i, j: (i, 0)),
            pl.BlockSpec((d, tf), lambda i, j: (0, j)),
            pl.BlockSpec((d, tf), lambda i, j: (0, j)),
            pl.BlockSpec((tf, d), lambda i, j: (j, 0)),
            pl.BlockSpec((1, d), lambda i, j: (0, 0)),
            pl.BlockSpec((1, d), lambda i, j: (0, 0)),
        ],
        out_specs=pl.BlockSpec((tm, d), lambda i, j: (i, 0)),
        out_shape=jax.ShapeDtypeStruct((n, d), F32),
        scratch_shapes=[pltpu.VMEM((tm, d), BF16), pltpu.VMEM((tm, d), F32)],
        compiler_params=_params(("parallel", "arbitrary")),
        name="ffn_ln",
    )(x, wg, wu, wd, g, b)


def _attn_proj_kernel(x_ref, wq_ref, wqi_ref, wkv_ref, wkw_ref,
                      q_ref, qi_ref, k_ref, v_ref, kb_ref, vb_ref, ki_ref, ki2_ref, wi_ref):
    xb = x_ref[...].astype(BF16)
    q_ref[...] = (_dot(xb, wq_ref[...]) * (HEAD_DIM ** -0.5 * LOG2E)).astype(BF16)
    qi_ref[...] = _dot(xb, wqi_ref[...]).astype(BF16)
    kv = _dot(xb, wkv_ref[...])
    nkv = kv.shape[1] // 2
    k_ref[...] = kv[:, :nkv]
    v_ref[...] = kv[:, nkv:]
    kb_ref[...] = kv[:, :nkv].astype(BF16)
    vb_ref[...] = kv[:, nkv:].astype(BF16)
    kw = _dot(xb, wkw_ref[...])
    ki_ref[...] = kw[:, :IDX_DIM]
    wi_ref[...] = kw[:, IDX_DIM:IDX_DIM + IDX_HEADS]
    lane = lax.broadcasted_iota(I32, kw.shape, 1)
    ki2_ref[...] = jnp.where(lane < IDX_DIM, kw, pltpu.roll(kw, IDX_DIM, 1)).astype(BF16)


def _attn_proj(x, wq, wqi, wkv, wkw, tm=256):
    n, d = x.shape
    tm = min(tm, n)
    assert n % tm == 0
    row = lambda w: pl.BlockSpec((tm, w), lambda i: (i, 0))
    full = lambda w: pl.BlockSpec(w.shape, lambda i: (0, 0))
    nq, nqi, nkv = wq.shape[1], wqi.shape[1], wkv.shape[1] // 2
    outs = [
        (nq, BF16), (nqi, BF16), (nkv, F32), (nkv, F32), (nkv, BF16), (nkv, BF16),
        (IDX_DIM, F32), (LANES, BF16), (IDX_HEADS, F32),
    ]
    return pl.pallas_call(
        _attn_proj_kernel,
        grid=(n // tm,),
        in_specs=[row(d), full(wq), full(wqi), full(wkv), full(wkw)],
        out_specs=[row(w) for w, _ in outs],
        out_shape=[jax.ShapeDtypeStruct((n, w), t) for w, t in outs],
        compiler_params=_params(("parallel",)),
        name="attn_proj",
    )(x, wq, wqi, wkv, wkw)


def _gla_proj_kernel(x_ref, wgq_ref, wgk_ref, wgv_ref, wgr_ref, wlr_ref, wa2_ref, ba_ref,
                     gq_ref, gk_ref, gv_ref, la_ref, gr_ref):
    xb = x_ref[...].astype(BF16)
    gq_ref[...] = _dot(xb, wgq_ref[...]) * (GLA_DK ** -0.5)
    gk_ref[...] = _dot(xb, wgk_ref[...])
    gv_ref[...] = _dot(xb, wgv_ref[...]).astype(BF16)
    gr_ref[...] = _dot(xb, wgr_ref[...])
    lr = _dot(xb, wlr_ref[...])
    z = _dot(lr.astype(BF16), wa2_ref[...]) + ba_ref[...]
    la_ref[...] = (jnp.minimum(z, 0.0) - jnp.log(1.0 + jnp.exp(-jnp.abs(z)))) * (1.0 / GLA_NORMALIZER)


def _gla_proj(x, wgq, wgk, wgv, wgr, wlr, wa2, ba, tm=256):
    n, d = x.shape
    tm = min(tm, n)
    assert n % tm == 0
    row = lambda w: pl.BlockSpec((tm, w), lambda i: (i, 0))
    full = lambda w: pl.BlockSpec(w.shape, lambda i: (0, 0))
    kw, vw = wgq.shape[1], wgv.shape[1]
    outs = [(kw, F32), (kw, F32), (vw, BF16), (kw, F32), (vw, F32)]
    return pl.pallas_call(
        _gla_proj_kernel,
        grid=(n // tm,),
        in_specs=[row(d), full(wgq), full(wgk), full(wgv), full(wgr), full(wlr), full(wa2), full(ba)],
        out_specs=[row(w) for w, _ in outs],
        out_shape=[jax.ShapeDtypeStruct((n, w), t) for w, t in outs],
        compiler_params=_params(("parallel",)),
        name="gla_proj",
    )(x, wgq, wgk, wgv, wgr, wlr, wa2, ba)


def _sortable_key(a):
    bits = lax.bitcast_convert_type(a + 0.0, I32)
    return bits ^ ((bits >> 31) & 0x7FFFFFFF)


def _dsa_kernel(q_ref, qi_ref, wi_ref, k_ref, v_ref, ki2_ref, bw_ref, o_ref,
                sc_ref, sc16_ref, wb_ref, qip_ref, thr_ref, qg_ref, acc_ref, m_ref, lg_ref, corr_ref,
                *, tq, cbs, cbf, qb0, n_sel):
    qb = pl.program_id(1) + qb0
    q0 = qb * LANES
    rows = GROUP * tq

    wi = wi_ref[0]
    for h in range(IDX_HEADS):
        wb_ref[h] = jnp.broadcast_to(wi[:, h:h + 1], (tq, LANES))
    qi = qi_ref[0]
    row = lax.broadcasted_iota(I32, (tq, LANES), 0)
    lane = lax.broadcasted_iota(I32, (tq, LANES), 1)
    for p in range(IDX_HEADS // 2):
        pair = qi[:, p * LANES:(p + 1) * LANES]
        qip_ref[0, p * tq:(p + 1) * tq, :] = jnp.where(lane < IDX_DIM, pair, jnp.zeros_like(pair))
        qip_ref[1, p * tq:(p + 1) * tq, :] = jnp.where(lane >= IDX_DIM, pair, jnp.zeros_like(pair))
    limit = q0 + (row // CHUNK + 1) * CHUNK
    tks = cbs * LANES

    def score_tile(kt, carry):
        start = pl.multiple_of(kt * tks, tks)
        ki2 = ki2_ref[0, pl.ds(start, tks), :]
        accs = [jnp.zeros((tq, LANES), F32) for _ in range(cbs)]
        for half in range(2):
            s = _dot_nt(qip_ref[half], ki2)
            for c in range(cbs):
                a = accs[c]
                for p in range(IDX_HEADS // 2):
                    sp = s[p * tq:(p + 1) * tq, c * LANES:(c + 1) * LANES]
                    a = a + jnp.maximum(sp, 0.0) * wb_ref[2 * p + half]
                accs[c] = a
        for c in range(cbs):
            adm = (kt * cbs + c) * LANES + lane < limit
            bits = lax.bitcast_convert_type(accs[c] + 0.0, I32)
            sc_ref[kt * cbs + c] = jnp.where(adm, bits ^ ((bits >> 31) & 0x7FFFFFFF), INT_MIN)
            top = lax.bitcast_convert_type(bits & -65536, F32)
            sc16_ref[kt * cbs + c] = jnp.where(adm, top, -jnp.inf).astype(BF16)
        return carry

    n_scored = _round_up(jnp.maximum(qb + 1, cbf), max(cbs, COUNT_UNROLL))
    lax.fori_loop(0, n_scored // cbs, score_tile, 0)

    one16 = jnp.ones((tq, LANES), BF16)
    zero16 = jnp.zeros((tq, LANES), BF16)

    def search(keys_ref, keys16_ref, n_groups):
        def upper_bit(i, state):
            tu, cnt_t = state
            cand_u = tu | lax.shift_left(jnp.int32(1), 15 - i)
            k32 = (jnp.clip(cand_u, BF16_NAN_CODES, 2 ** 16 - BF16_NAN_CODES) - 2 ** 15) << 16
            cval = lax.bitcast_convert_type(k32 ^ ((k32 >> 31) & 0x7FFF0000), F32)
            cand_b = jnp.broadcast_to(cval, (tq, LANES)).astype(BF16)

            def count(gi, acc):
                for u in range(COUNT_UNROLL):
                    acc = acc + jnp.where(keys16_ref[gi * COUNT_UNROLL + u] >= cand_b, one16, zero16)
                return acc

            acc = lax.fori_loop(0, n_groups, count, zero16)
            cnt = jnp.sum(acc.astype(F32), axis=1, keepdims=True)
            ok = cnt >= n_sel
            return jnp.where(ok, cand_u, tu), jnp.where(ok, cnt, cnt_t)

        tu, cnt_t = lax.fori_loop(0, 16, upper_bit,
                                  (jnp.zeros((tq, 1), I32), jnp.full((tq, 1), float(2 ** 30), F32)))

        def lower_bit(state):
            i, t, cnt_t = state
            cand = t | lax.shift_left(jnp.int32(1), 15 - i)
            cand_b = jnp.broadcast_to(cand, (tq, LANES))

            def count(gi, acc):
                for u in range(COUNT_UNROLL):
                    acc = acc + jnp.where(keys_ref[gi * COUNT_UNROLL + u] >= cand_b, 1.0, 0.0)
                return acc

            acc = lax.fori_loop(0, n_groups, count, jnp.zeros((tq, LANES), F32))
            cnt = jnp.sum(acc, axis=1, keepdims=True)
            ok = cnt >= n_sel
            return i + 1, jnp.where(ok, cand, t), jnp.where(ok, cnt, cnt_t)

        def undecided(state):
            i, _, cnt_t = state
            return (i < 16) & (jnp.max(jnp.abs(cnt_t - n_sel)) > 0.0)

        _, t, cnt_t = lax.while_loop(undecided, lower_bit, (jnp.int32(0), (tu - 2 ** 15) << 16, cnt_t))
        return t, cnt_t

    t, cnt_t = search(sc_ref, sc16_ref, _round_up(qb + 1, COUNT_UNROLL) // COUNT_UNROLL)
    thr_ref[...] = jnp.broadcast_to(jnp.maximum(t, INT_MIN + 1), (tq, LANES))

    tied = (cnt_t > n_sel) & (cnt_t < float(2 ** 29))

    @pl.when(jnp.max(jnp.where(tied, 1.0, 0.0)) > 0.0)
    def _():
        t_b = jnp.broadcast_to(t, (tq, LANES))
        tied_b = jnp.broadcast_to(jnp.where(tied, 1, 0), (tq, LANES)) > 0

        def count_where(pred):
            def body(tt, acc):
                return acc + jnp.where(pred(sc_ref[tt], tt * LANES + lane), 1.0, 0.0)
            acc = lax.fori_loop(0, qb + 1, body, jnp.zeros((tq, LANES), F32))
            return jnp.sum(acc, axis=1, keepdims=True)

        keep = n_sel - count_where(lambda key, kpos: key > t_b)
        n_bits = max(1, (sc_ref.shape[0] * LANES - 1).bit_length())

        def position_bit(i, cut):
            cand = cut | lax.shift_left(jnp.int32(1), n_bits - 1 - i)
            cand_b = jnp.broadcast_to(cand, (tq, LANES))
            cnt = count_where(lambda key, kpos: (key == t_b) & (kpos < cand_b))
            return jnp.where(cnt <= keep, cand, cut)

        cut = lax.fori_loop(0, n_bits, position_bit, jnp.zeros((tq, 1), I32))
        cut_b = jnp.broadcast_to(cut, (tq, LANES))

        def demote(tt, carry):
            key = sc_ref[tt]
            late = tied_b & (key == t_b) & (tt * LANES + lane >= cut_b)
            sc_ref[tt] = jnp.where(late, key - 1, key)
            return carry

        lax.fori_loop(0, qb + 1, demote, 0)

    q = q_ref[0]
    thr = thr_ref[...]
    vw = 2 * HEAD_DIM

    def values(span, kh):
        vt = v_ref[0, span, kh * HEAD_DIM:(kh + 1) * HEAD_DIM]
        one_col = lax.broadcasted_iota(I32, vt.shape, 1) == 0
        return jnp.concatenate([vt, jnp.where(one_col, 1.0, 0.0).astype(BF16)], axis=1)

    def flash_update(kh, logits, vt):
        m_prev = m_ref[kh]
        m_new = jnp.maximum(m_prev, jnp.max(logits, axis=1, keepdims=True))
        p = jnp.exp2(logits - m_new)
        acc_ref[kh] = jnp.exp2(m_prev - m_new) * acc_ref[kh] + _dot(p.astype(BF16), vt)
        m_ref[kh] = m_new

    for kh in range(N_KV_HEADS):
        for g in range(GROUP):
            c0 = (kh * GROUP + g) * HEAD_DIM
            qg_ref[kh, g * tq:(g + 1) * tq, :] = q[:, c0:c0 + HEAD_DIM]
        m_ref[kh] = jnp.full((rows, 1), NEG_BIG, F32)
        acc_ref[kh] = jnp.zeros((rows, vw), F32)

    def drop(tt, extra=0.0):
        return jnp.concatenate([jnp.where(sc_ref[tt] >= thr, 0.0, NEG_BIG) + extra] * GROUP, axis=0)

    def attend(start, width, dropped, bias):
        for kh in range(N_KV_HEADS):
            kt_h = k_ref[0, pl.ds(start, width), kh * HEAD_DIM:(kh + 1) * HEAD_DIM]
            logits = _dot_nt(qg_ref[kh], kt_h) + dropped
            if bias is not None:
                logits = logits + bias[kh]
            flash_update(kh, logits, values(pl.ds(start, width), kh))

    n_free = jnp.maximum(qb - 1, 0)

    n_far = jnp.maximum((n_free + cbf - 1) // cbf, 1)
    wf = cbf * LANES

    def far_span(j):
        t_hi = n_free - j * cbf
        t_lo = jnp.maximum(t_hi - cbf, 0)
        return t_lo, t_hi, pl.ds(pl.multiple_of(t_lo * LANES, LANES), wf)

    def far_logits(j):
        t_lo, t_hi, span = far_span(j)
        parts = [drop(t_lo + c, jnp.where(t_lo + c < t_hi, 0.0, NEG_BIG)) for c in range(cbf)]
        dropped = jnp.concatenate(parts, axis=1)
        for kh in range(N_KV_HEADS):
            logits = _dot_nt(qg_ref[kh], k_ref[0, span, kh * HEAD_DIM:(kh + 1) * HEAD_DIM]) + dropped
            m_prev = m_ref[kh]
            m_new = jnp.maximum(m_prev, jnp.max(logits, axis=1, keepdims=True))
            lg_ref[j % 2, kh] = logits
            corr_ref[j % 2, kh, 0] = jnp.exp2(m_prev - m_new)
            corr_ref[j % 2, kh, 1] = m_new
            m_ref[kh] = m_new

    def far_values(j):
        _, _, span = far_span(j)
        for kh in range(N_KV_HEADS):
            p = jnp.exp2(lg_ref[j % 2, kh] - corr_ref[j % 2, kh, 1]).astype(BF16)
            acc_ref[kh] = corr_ref[j % 2, kh, 0] * acc_ref[kh] + _dot(p, values(span, kh))

    t0 = jnp.maximum(qb - 1, 0)
    variant = jnp.where(qb == 0, 1, 0)
    attend(pl.multiple_of(t0 * LANES, LANES), 2 * LANES, jnp.concatenate([drop(t0), drop(t0 + 1)], axis=1),
           [bw_ref[variant, kh] for kh in range(N_KV_HEADS)])
    far_logits(0)

    def far_step(j, carry):
        far_values(j)
        far_logits(j + 1)
        return carry

    lax.fori_loop(0, n_far - 1, far_step, 0)
    far_values(n_far - 1)

    for kh in range(N_KV_HEADS):
        acc = acc_ref[kh]
        o = acc[:, :HEAD_DIM] / acc[:, HEAD_DIM:HEAD_DIM + 1]
        for g in range(GROUP):
            c0 = (kh * GROUP + g) * HEAD_DIM
            o_ref[0, :, c0:c0 + HEAD_DIM] = o[g * tq:(g + 1) * tq].astype(o_ref.dtype)


def _dsa(q, qi, wi, kb, vb, ki2, bw, *, tq, cbs, cbf, qb0, n_sel):
    bsz, t, aw = q.shape
    length = kb.shape[1]
    assert t % tq == 0 and length % (LANES * max(cbs, COUNT_UNROLL)) == 0 and length >= cbf * LANES
    n_tiles = length // LANES
    rows = GROUP * tq
    qspec = lambda w: pl.BlockSpec((1, tq, w), lambda b, i: (b, i, 0))
    kspec = lambda w: pl.BlockSpec((1, length, w), lambda b, i: (b, 0, 0), pipeline_mode=pl.Buffered(1))
    return pl.pallas_call(
        functools.partial(_dsa_kernel, tq=tq, cbs=cbs, cbf=cbf, qb0=qb0, n_sel=n_sel),
        grid=(bsz, t // tq),
        in_specs=[
            qspec(aw), qspec(qi.shape[2]), qspec(wi.shape[2]),
            kspec(kb.shape[2]), kspec(vb.shape[2]), kspec(ki2.shape[2]),
            pl.BlockSpec(bw.shape, lambda b, i: (0, 0, 0, 0)),
        ],
        out_specs=qspec(aw),
        out_shape=jax.ShapeDtypeStruct((bsz, t, aw), BF16),
        scratch_shapes=[
            pltpu.VMEM((n_tiles, tq, LANES), I32),
            pltpu.VMEM((n_tiles, tq, LANES), BF16),
            pltpu.VMEM((IDX_HEADS, tq, LANES), F32),
            pltpu.VMEM((2, IDX_HEADS // 2 * tq, LANES), BF16),
            pltpu.VMEM((tq, LANES), I32),
            pltpu.VMEM((N_KV_HEADS, rows, HEAD_DIM), BF16),
            pltpu.VMEM((N_KV_HEADS, rows, 2 * HEAD_DIM), F32),
            pltpu.VMEM((N_KV_HEADS, rows, 1), F32),
            pltpu.VMEM((2, N_KV_HEADS, rows, cbf * LANES), F32),
            pltpu.VMEM((2, N_KV_HEADS, 2, rows, 1), F32),
        ],
        compiler_params=_params(("parallel", "arbitrary")),
        name="dsa",
    )(q, qi, wi, kb, vb, ki2, bw)


def _rel_bucket(rel):
    half = N_BUCKETS // 2
    max_exact = half // 2
    ret = jnp.where(rel > 0, half, 0)
    n = jnp.abs(rel)
    nf = jnp.maximum(n, 1).astype(F32)
    large = max_exact + (jnp.log(nf / max_exact) / math.log(MAX_DISTANCE / max_exact)
                         * (half - max_exact)).astype(I32)
    large = jnp.minimum(large, half - 1)
    return ret + jnp.where(n < max_exact, n, large)


def _bias_window(rel_bias, tq):
    i = jnp.arange(tq, dtype=I32)[:, None]
    c = jnp.arange(2 * LANES, dtype=I32)[None, :]
    rel = c - LANES - i
    far = rel_bias[N_BUCKETS // 2 - 1]
    bias = (rel_bias[_rel_bucket(rel)] - far) * LOG2E
    bias = jnp.where((rel <= -MAX_DISTANCE)[:, :, None], 0.0, bias)
    bias = bias.transpose(2, 0, 1).reshape(N_KV_HEADS, GROUP * tq, 2 * LANES)
    first = jnp.concatenate([bias[..., LANES:], jnp.zeros_like(bias[..., LANES:])], axis=-1)
    return jnp.stack([bias, first]).astype(F32)


def _split3(x):
    hi = x.astype(BF16)
    r = x - hi.astype(F32)
    mid = r.astype(BF16)
    lo = (r - mid.astype(F32)).astype(BF16)
    return hi, mid, lo


def _gla_kernel(gq_ref, gk_ref, gv_ref, la_ref, gr_ref, g_ref, s0_ref, o_ref, sf_ref, st_ref, a_ref, b_ref,
                *, chunk, n_chunks):
    step = pl.program_id(1)

    @pl.when(step == 0)
    def _():
        for h in range(GLA_HEADS):
            st_ref[h] = s0_ref[0, h].astype(F32).T

    ri = lax.broadcasted_iota(I32, (chunk, chunk), 0)
    ci = lax.broadcasted_iota(I32, (chunk, chunk), 1)
    tri = jnp.where(ci <= ri, 1.0, 0.0).astype(BF16)
    rk = lax.broadcasted_iota(I32, (chunk, GLA_DK), 0)
    levels = []
    size = chunk // 2
    while size >= GLA_SUB:
        levels.append(size)
        size //= 2

    def ref_rows(b, idx):
        out = []
        start = 0
        while start < chunk:
            end = start
            while end < chunk and idx[end] == idx[start]:
                end += 1
            out.append(jnp.broadcast_to(b[idx[start]:idx[start] + 1, :], (end - start, b.shape[1])))
            start = end
        return jnp.concatenate(out, axis=0)

    def row_of(rows8, j):
        pick = lax.broadcasted_iota(I32, rows8.shape, 0) == j % SUBLANES
        return jnp.sum(jnp.where(pick, rows8, 0.0), axis=0, keepdims=True)

    def chunk_heads(rs, exact_diagonal):
        for h in range(GLA_HEADS):
            ks = slice(h * GLA_DK, (h + 1) * GLA_DK)
            vs = slice(h * GLA_DV, (h + 1) * GLA_DV)
            qc = gq_ref[0, rs, ks]
            kc = gk_ref[0, rs, ks]
            vc = gv_ref[0, rs, vs]
            hi, mid, lo = _split3(la_ref[0, rs, ks])
            b = _dot(tri, hi) + _dot(tri, mid) + _dot(tri, lo)
            st_prev = st_ref[h]

            keep = (ri // GLA_SUB == ci // GLA_SUB) & (ci <= ri)
            if exact_diagonal:
                a_ref[...] = jnp.zeros_like(a_ref)
                b_ref[...] = b

                def column(j, carry):
                    grp = pl.multiple_of(j // SUBLANES * SUBLANES, SUBLANES)
                    bj = row_of(b_ref[pl.ds(grp, SUBLANES), :], j)
                    kj = row_of(gk_ref[0, pl.ds(rs.start + grp, SUBLANES), ks], j)
                    pair = qc * kj * jnp.exp(jnp.minimum(b - bj, 0.0))
                    col = jnp.sum(pair, axis=1, keepdims=True)
                    a_ref[...] = jnp.where(keep & (ci == j), col, a_ref[...])
                    return carry

                lax.fori_loop(0, chunk, column, 0)
                a = a_ref[...]
            else:
                bref = ref_rows(b, [(r // GLA_SUB) * GLA_SUB for r in range(chunk)])
                qd = (qc * jnp.exp(b - bref)).astype(BF16)
                kd = (kc * jnp.exp(bref - b)).astype(BF16)
                a = jnp.where(keep, _dot_nt(qd, kd), 0.0)
            for size in levels:
                bref = ref_rows(b, [(r // (2 * size)) * 2 * size + size - 1 for r in range(chunk)])
                upper = (rk // size) % 2 == 1
                ql = (qc * jnp.exp(jnp.where(upper, b - bref, 0.0))).astype(BF16)
                kl = (kc * jnp.exp(jnp.where(upper, 0.0, bref - b))).astype(BF16)
                split = (ri // (2 * size) == ci // (2 * size)) & ((ri // size) % 2 == 1) & ((ci // size) % 2 == 0)
                a = jnp.where(split, _dot_nt(ql, kl), a)

            o = _dot_nt((qc * jnp.exp(b)).astype(BF16), st_prev.astype(BF16)) + _dot(a.astype(BF16), vc)
            b_last = b[chunk - 1:chunk, :]
            kdec = (kc * jnp.exp(b_last - b)).astype(BF16)
            st_ref[h] = jnp.exp(b_last) * st_prev + lax.dot_general(
                vc, kdec, (((0,), (0,)), ((), ())), preferred_element_type=F32)

            mu = jnp.mean(o, axis=-1, keepdims=True)
            d = o - mu
            var = jnp.mean(d * d, axis=-1, keepdims=True)
            gr = gr_ref[0, rs, vs]
            y = d * lax.rsqrt(var + LN_EPS) * g_ref[h] * (gr * jax.nn.sigmoid(gr))
            o_ref[0, rs, vs] = y.astype(o_ref.dtype)

    n_blocks = _round_up(chunk * n_chunks // GLA_SUB, 2 * SUBLANES)
    bi = lax.broadcasted_iota(I32, (n_blocks, chunk * n_chunks), 0)
    rj = lax.broadcasted_iota(I32, (n_blocks, chunk * n_chunks), 1)
    in_block = jnp.where((rj // GLA_SUB == bi) & (rj % GLA_SUB != 0), 1.0, 0.0).astype(BF16)
    steep = jnp.max(_dot(in_block, (-la_ref[0]).astype(BF16))) > GLA_SAFE_EXP

    def chunks(exact_diagonal):
        def chunk_step(c, carry):
            chunk_heads(pl.ds(pl.multiple_of(c * chunk, chunk), chunk), exact_diagonal)
            return carry

        lax.fori_loop(0, n_chunks, chunk_step, 0)

    @pl.when(jnp.logical_not(steep))
    def _():
        chunks(False)

    @pl.when(steep)
    def _():
        chunks(True)

    @pl.when(step == pl.num_programs(1) - 1)
    def _():
        for h in range(GLA_HEADS):
            sf_ref[0, h] = st_ref[h].T.astype(sf_ref.dtype)


def _gla(gq, gk, gv, la, gr, g, s0, *, chunk, n_chunks):
    bsz, t, kw = gq.shape
    vw = gv.shape[2]
    tb = chunk * n_chunks
    assert t % tb == 0
    tok = lambda w: pl.BlockSpec((1, tb, w), lambda b, i: (b, i, 0))
    sspec = pl.BlockSpec((1,) + s0.shape[1:], lambda b, i: (b, 0, 0, 0))
    return pl.pallas_call(
        functools.partial(_gla_kernel, chunk=chunk, n_chunks=n_chunks),
        grid=(bsz, t // tb),
        in_specs=[tok(kw), tok(kw), tok(vw), tok(kw), tok(vw),
                  pl.BlockSpec(g.shape, lambda b, i: (0, 0, 0)), sspec],
        out_specs=[tok(vw), sspec],
        out_shape=[jax.ShapeDtypeStruct((bsz, t, vw), BF16), jax.ShapeDtypeStruct(s0.shape, s0.dtype)],
        scratch_shapes=[pltpu.VMEM((s0.shape[1], s0.shape[3], s0.shape[2]), F32),
                        pltpu.VMEM((chunk, chunk), F32), pltpu.VMEM((chunk, GLA_DK), F32)],
        compiler_params=_params(("parallel", "arbitrary")),
        name="gla",
    )(gq, gk, gv, la, gr, g, s0)


def _out_ln_kernel(x_ref, a_ref, gl_ref, woa_ref, wog_ref, g_ref, b_ref, o_ref, *, alpha):
    mix = _dot(a_ref[...], woa_ref[...]) + _dot(gl_ref[...], wog_ref[...])
    o_ref[...] = _post_norm(x_ref[...], mix, g_ref[...], b_ref[...], alpha)


def _out_ln(x, attn, gla, woa, wog, g, b, alpha, tm=512):
    n, d = x.shape
    tm = min(tm, n)
    assert n % tm == 0
    row = lambda w: pl.BlockSpec((tm, w), lambda i: (i, 0))
    full = lambda w: pl.BlockSpec(w.shape, lambda i: (0, 0))
    return pl.pallas_call(
        functools.partial(_out_ln_kernel, alpha=alpha),
        grid=(n // tm,),
        in_specs=[row(d), row(attn.shape[1]), row(gla.shape[1]), full(woa), full(wog), full(g), full(b)],
        out_specs=row(d),
        out_shape=jax.ShapeDtypeStruct((n, d), F32),
        compiler_params=_params(("parallel",)),
        name="out_ln",
    )(x, attn, gla, woa, wog, g, b)


def _matmul_kernel(x_ref, w_ref, o_ref):
    o_ref[...] = _dot(x_ref[...].astype(BF16), w_ref[...]).astype(o_ref.dtype)


def _matmul(x, w, tm=256, tn=512):
    n, d = x.shape
    m = w.shape[1]
    tm, tn = min(tm, n), min(tn, m)
    assert n % tm == 0 and m % tn == 0
    return pl.pallas_call(
        _matmul_kernel,
        grid=(n // tm, m // tn),
        in_specs=[pl.BlockSpec((tm, d), lambda i, j: (i, 0)), pl.BlockSpec((d, tn), lambda i, j: (0, j))],
        out_specs=pl.BlockSpec((tm, tn), lambda i, j: (i, j)),
        out_shape=jax.ShapeDtypeStruct((n, m), F32),
        compiler_params=_params(("parallel", "parallel")),
        name="matmul",
    )(x, w)


def _mem_ln_kernel(x_ref, wq_ref, mk_ref, mv_ref, wo_ref, g_ref, b_ref, o_ref, *, alpha):
    x = x_ref[0]
    q = _dot(x.astype(BF16), wq_ref[...])
    scale = MEM_HEAD_DIM ** -0.5
    heads = []
    for h in range(MEM_HEADS):
        hs = slice(h * MEM_HEAD_DIM, (h + 1) * MEM_HEAD_DIM)
        logits = _dot_nt(q[:, hs].astype(BF16), mk_ref[0, :, hs]) * scale
        m = jnp.max(logits, axis=-1, keepdims=True)
        p = jnp.exp(logits - m)
        p = p / jnp.sum(p, axis=-1, keepdims=True)
        heads.append(_dot(p.astype(BF16), mv_ref[0, :, hs]))
    o = jnp.concatenate(heads, axis=1).astype(BF16)
    o_ref[0] = _post_norm(x, _dot(o, wo_ref[...]), g_ref[...], b_ref[...], alpha)


def _mem_ln(x, wq, mk, mv, wo, g, b, alpha, tm=512):
    bsz, t, d = x.shape
    tm = min(tm, t)
    assert t % tm == 0
    full = lambda w: pl.BlockSpec(w.shape, lambda bb, i: (0, 0))
    mem = pl.BlockSpec((1,) + mk.shape[1:], lambda bb, i: (bb, 0, 0))
    tok = pl.BlockSpec((1, tm, d), lambda bb, i: (bb, i, 0))
    return pl.pallas_call(
        functools.partial(_mem_ln_kernel, alpha=alpha),
        grid=(bsz, t // tm),
        in_specs=[tok, full(wq), mem, mem, full(wo), full(g), full(b)],
        out_specs=tok,
        out_shape=jax.ShapeDtypeStruct((bsz, t, d), F32),
        compiler_params=_params(("parallel", "parallel")),
        name="mem_ln",
    )(x, wq, mk, mv, wo, g, b)


def _split_w_in(w_in, w_a2, b_a):
    attn_w = N_HEADS * HEAD_DIM
    kv_w = N_KV_HEADS * HEAD_DIM
    gla_kw = GLA_HEADS * GLA_DK
    gla_vw = GLA_HEADS * GLA_DV
    widths = (attn_w, kv_w, kv_w, IDX_HEADS * IDX_DIM, IDX_DIM, IDX_HEADS, gla_kw, gla_kw, gla_vw, GLA_RANK, gla_vw)
    assert sum(widths) == w_in.shape[1]
    offs = [0]
    for w in widths:
        offs.append(offs[-1] + w)
    col = lambda a, b: w_in[:, offs[a]:offs[b]].astype(BF16)
    wq, wkv, wqi = col(0, 1), col(1, 3), col(3, 4)
    wkw = jnp.pad(col(4, 6), ((0, 0), (0, LANES - IDX_DIM - IDX_HEADS)))
    wgq, wgk, wgv, wgr = col(6, 7), col(7, 8), col(8, 9), col(10, 11)
    wlr = jnp.pad(col(9, 10), ((0, 0), (0, LANES - GLA_RANK)))
    wa2 = jnp.pad(w_a2.astype(BF16), ((0, LANES - GLA_RANK), (0, 0)))
    return (wq, wqi, wkv, wkw), (wgq, wgk, wgv, wgr, wlr, wa2, b_a.reshape(1, -1).astype(F32))


def kernel(x_prompt, x_sample, cache_k, cache_v, cache_idx_k, state_gla, cache_mem_k, cache_mem_v, mem_prompt,
           rel_bias, ln_g, ln_b, ffn1_wg, ffn1_wu, ffn1_wd, w_in, w_a2, b_a, gla_norm_g, w_o, w_mq, w_mk, w_mv,
           w_mo, ffn2_wg, ffn2_wu, ffn2_wd):
    depth = w_in.shape[0]
    bp, tp, d = x_prompt.shape
    bs, ts, _ = x_sample.shape
    past = cache_k.shape[2]
    alpha = (2.0 * depth) ** 0.25
    attn_w = N_HEADS * HEAD_DIM
    kv_w = N_KV_HEADS * HEAD_DIM
    mem_w = MEM_HEADS * MEM_HEAD_DIM
    assert tp % LANES == 0 and ts == CHUNK and past % LANES == 0

    xp = x_prompt.reshape(bp * tp, d)
    xs = x_sample.reshape(bs * ts, d)
    pk, pv, pki, pS, pmk, pmv = [], [], [], [], [], []
    sk, sv, ski, sS = [], [], [], []
    bw_p = _bias_window(rel_bias, LANES)
    bw_s = bw_p.reshape(2, N_KV_HEADS, GROUP, LANES, 2 * LANES)[:, :, :, :ts].reshape(2, N_KV_HEADS, GROUP * ts, 2 * LANES)
    for l in range(depth):
        g = lambda i: ln_g[l, i].reshape(1, d)
        b = lambda i: ln_b[l, i].reshape(1, d)
        bf = lambda w: w[l].astype(BF16)
        f1 = (bf(ffn1_wg), bf(ffn1_wu), bf(ffn1_wd))
        f2 = (bf(ffn2_wg), bf(ffn2_wu), bf(ffn2_wd))
        w_attn, w_gla = _split_w_in(w_in[l], w_a2[l], b_a[l])
        woa, wog = w_o[l, :attn_w].astype(BF16), w_o[l, attn_w:].astype(BF16)
        gn = gla_norm_g[l].reshape(GLA_HEADS, 1, GLA_DV).astype(F32)

        xp = _ffn_ln(xp, *f1, g(0), b(0), alpha)
        xs = _ffn_ln(xs, *f1, g(0), b(0), alpha)

        q, qi, k, v, kb, vb, ki, ki2, wi = _attn_proj(xp, *w_attn)
        r3 = lambda a, n=bp, t=tp: a.reshape(n, t, a.shape[-1])
        attn = _dsa(r3(q), r3(qi), r3(wi), r3(kb), r3(vb), r3(ki2), bw_p,
                    tq=LANES, cbs=DSA_SCORE_TILES, cbf=DSA_FAR_TILES, qb0=0, n_sel=min(TOPK_MAX, tp // 4))
        gq, gk, gv, la, gr = _gla_proj(xp, *w_gla)
        s0 = jnp.zeros((bp, GLA_HEADS, GLA_DK, GLA_DV), state_gla.dtype)
        go, s_p = _gla(r3(gq), r3(gk), r3(gv), r3(la), r3(gr), gn, s0, chunk=CHUNK, n_chunks=4)
        xp = _out_ln(xp, attn.reshape(bp * tp, attn_w), go.reshape(bp * tp, -1), woa, wog, g(1), b(1), alpha)
        pk.append(k.reshape(bp, tp, N_KV_HEADS, HEAD_DIM))
        pv.append(v.reshape(bp, tp, N_KV_HEADS, HEAD_DIM))
        pki.append(ki.reshape(bp, tp, IDX_DIM))
        pS.append(s_p)

        q, qi, k, v, kb, vb, ki, ki2, wi = _attn_proj(xs, *w_attn)
        r3s = lambda a: a.reshape(bs, ts, a.shape[-1])
        pad_t = (-(past + ts)) % (LANES * max(DSA_SCORE_TILES, COUNT_UNROLL))
        cat = lambda c, n: jnp.pad(jnp.concatenate([c.astype(BF16), r3s(n)], axis=1), ((0, 0), (0, pad_t), (0, 0)))
        k_all = cat(cache_k[l].reshape(bs, past, kv_w), kb)
        v_all = cat(cache_v[l].reshape(bs, past, kv_w), vb)
        ki2_all = cat(jnp.concatenate([cache_idx_k[l]] * 2, axis=-1), ki2)
        attn = _dsa(r3s(q), r3s(qi), r3s(wi), k_all, v_all, ki2_all, bw_s,
                    tq=ts, cbs=DSA_SCORE_TILES, cbf=DSA_FAR_TILES, qb0=past // LANES,
                    n_sel=min(TOPK_MAX, (past + ts) // 4))
        gq, gk, gv, la, gr = _gla_proj(xs, *w_gla)
        go, s_s = _gla(r3s(gq), r3s(gk), r3s(gv), r3s(la), r3s(gr), gn, state_gla[l], chunk=ts, n_chunks=1)
        xs = _out_ln(xs, attn.reshape(bs * ts, attn_w), go.reshape(bs * ts, -1), woa, wog, g(1), b(1), alpha)
        sk.append(k.reshape(bs, ts, N_KV_HEADS, HEAD_DIM))
        sv.append(v.reshape(bs, ts, N_KV_HEADS, HEAD_DIM))
        ski.append(ki.reshape(bs, ts, IDX_DIM))
        sS.append(s_s)

        n_mem = mem_prompt.shape[1]
        mkv = _matmul(mem_prompt.reshape(bp * n_mem, d), jnp.concatenate([bf(w_mk), bf(w_mv)], axis=1))
        mk_p = mkv[:, :mem_w].reshape(bp, n_mem, mem_w)
        mv_p = mkv[:, mem_w:].reshape(bp, n_mem, mem_w)
        wmq, wmo = bf(w_mq), bf(w_mo)
        xp = _mem_ln(xp.reshape(bp, tp, d), wmq, mk_p.astype(BF16), mv_p.astype(BF16), wmo, g(2), b(2),
                     alpha).reshape(bp * tp, d)
        xs = _mem_ln(xs.reshape(bs, ts, d), wmq, cache_mem_k[l].reshape(bs, n_mem, mem_w).astype(BF16),
                     cache_mem_v[l].reshape(bs, n_mem, mem_w).astype(BF16), wmo, g(2), b(2),
                     alpha).reshape(bs * ts, d)
        pmk.append(mk_p.reshape(bp, n_mem, MEM_HEADS, MEM_HEAD_DIM))
        pmv.append(mv_p.reshape(bp, n_mem, MEM_HEADS, MEM_HEAD_DIM))

        xp = _ffn_ln(xp, *f2, g(3), b(3), alpha)
        xs = _ffn_ln(xs, *f2, g(3), b(3), alpha)

    return (xp.reshape(bp, tp, d), xs.reshape(bs, ts, d),
            jnp.stack(pk), jnp.stack(pv), jnp.stack(pki), jnp.stack(pS), jnp.stack(pmk), jnp.stack(pmv),
            jnp.stack(sk), jnp.stack(sv), jnp.stack(ski), jnp.stack(sS))
```

```python
import functools
import math

import jax
import jax.numpy as jnp
from jax import lax
from jax.experimental import pallas as pl
from jax.experimental.pallas import tpu as pltpu

F32 = jnp.float32
BF16 = jnp.bfloat16
I32 = jnp.int32

LANES = 128
SUBLANES = 8
VMEM_BYTES_V7X = 64 * 1024 * 1024
VMEM_LIMIT = VMEM_BYTES_V7X - 4 * 1024 * 1024

CHUNK = 64
N_HEADS = 8
N_KV_HEADS = 2
HEAD_DIM = 128
GROUP = N_HEADS // N_KV_HEADS
IDX_HEADS = 16
IDX_DIM = 64
TOPK_MAX = 256
GLA_HEADS = 4
GLA_DK = 128
GLA_DV = 256
GLA_RANK = 16
GLA_NORMALIZER = 16.0
MEM_HEADS = 4
MEM_HEAD_DIM = 128
N_BUCKETS = 32
MAX_DISTANCE = 128
LN_EPS = 1e-5
INT_MIN = -2 ** 31
NEG_BIG = -1e30
GLA_SUB = 16
GLA_SAFE_EXP = 60.0
BF16_NAN_CODES = 2 ** 7
COUNT_UNROLL = 8
DSA_SCORE_TILES = 8
DSA_FAR_TILES = 8
LOG2E = math.log2(math.e)


def _round_up(x, m):
    return (x + m - 1) // m * m


def _dot(a, b):
    return jnp.dot(a, b, preferred_element_type=F32)


def _dot_nt(a, b):
    return lax.dot_general(a, b, (((1,), (1,)), ((), ())), preferred_element_type=F32)


def _params(semantics):
    return pltpu.CompilerParams(dimension_semantics=semantics, vmem_limit_bytes=VMEM_LIMIT)


def _post_norm(x, sub, g, b, alpha):
    h = alpha * x + sub
    mu = jnp.mean(h, axis=-1, keepdims=True)
    d = h - mu
    var = jnp.mean(d * d, axis=-1, keepdims=True)
    return d * lax.rsqrt(var + LN_EPS) * g + b


def _ffn_ln_kernel(x_ref, wg_ref, wu_ref, wd_ref, g_ref, b_ref, o_ref, xb_ref, acc_ref, *, alpha):
    j = pl.program_id(1)

    @pl.when(j == 0)
    def _():
        xb_ref[...] = x_ref[...].astype(BF16)
        acc_ref[...] = jnp.zeros_like(acc_ref)

    xb = xb_ref[...]
    hg = _dot(xb, wg_ref[...])
    hu = _dot(xb, wu_ref[...])
    h = (hg * jax.nn.sigmoid(hg)) * hu
    acc_ref[...] += _dot(h.astype(BF16), wd_ref[...])

    @pl.when(j == pl.num_programs(1) - 1)
    def _():
        o_ref[...] = _post_norm(x_ref[...], 0.5 * acc_ref[...], g_ref[...], b_ref[...], alpha)


def _ffn_ln(x, wg, wu, wd, g, b, alpha, tm=512, tf=512):
    n, d = x.shape
    f = wg.shape[1]
    tm = min(tm, n)
    assert n % tm == 0 and f % tf == 0
    return pl.pallas_call(
        functools.partial(_ffn_ln_kernel, alpha=alpha),
        grid=(n // tm, f // tf),
        in_specs=[
            pl.BlockSpec((tm, d), lambda i, j: (i, 0)),
            pl.BlockSpec((d, tf), lambda i, j: (0, j)),
            pl.BlockSpec((d, tf), lambda i, j: (0, j)),
            pl.BlockSpec((tf, d), lambda i, j: (j, 0)),
            pl.BlockSpec((1, d), lambda i, j: (0, 0)),
            pl.BlockSpec((1, d), lambda i, j: (0, 0)),
        ],
        out_specs=pl.BlockSpec((tm, d), lambda i, j: (i, 0)),
        out_shape=jax.ShapeDtypeStruct((n, d), F32),
        scratch_shapes=[pltpu.VMEM((tm, d), BF16), pltpu.VMEM((tm, d), F32)],
        compiler_params=_params(("parallel", "arbitrary")),
        name="ffn_ln",
    )(x, wg, wu, wd, g, b)


def _attn_proj_kernel(x_ref, wq_ref, wqi_ref, wkv_ref, wkw_ref,
                      q_ref, qi_ref, k_ref, v_ref, kb_ref, vb_ref, ki_ref, ki2_ref, wi_ref):
    xb = x_ref[...].astype(BF16)
    q_ref[...] = (_dot(xb, wq_ref[...]) * (HEAD_DIM ** -0.5 * LOG2E)).astype(BF16)
    qi_ref[...] = _dot(xb, wqi_ref[...]).astype(BF16)
    kv = _dot(xb, wkv_ref[...])
    nkv = kv.shape[1] // 2
    k_ref[...] = kv[:, :nkv]
    v_ref[...] = kv[:, nkv:]
    kb_ref[...] = kv[:, :nkv].astype(BF16)
    vb_ref[...] = kv[:, nkv:].astype(BF16)
    kw = _dot(xb, wkw_ref[...])
    ki_ref[...] = kw[:, :IDX_DIM]
    wi_ref[...] = kw[:, IDX_DIM:IDX_DIM + IDX_HEADS]
    lane = lax.broadcasted_iota(I32, kw.shape, 1)
    ki2_ref[...] = jnp.where(lane < IDX_DIM, kw, pltpu.roll(kw, IDX_DIM, 1)).astype(BF16)


def _attn_proj(x, wq, wqi, wkv, wkw, tm=256):
    n, d = x.shape
    tm = min(tm, n)
    assert n % tm == 0
    row = lambda w: pl.BlockSpec((tm, w), lambda i: (i, 0))
    full = lambda w: pl.BlockSpec(w.shape, lambda i: (0, 0))
    nq, nqi, nkv = wq.shape[1], wqi.shape[1], wkv.shape[1] // 2
    outs = [
        (nq, BF16), (nqi, BF16), (nkv, F32), (nkv, F32), (nkv, BF16), (nkv, BF16),
        (IDX_DIM, F32), (LANES, BF16), (IDX_HEADS, F32),
    ]
    return pl.pallas_call(
        _attn_proj_kernel,
        grid=(n // tm,),
        in_specs=[row(d), full(wq), full(wqi), full(wkv), full(wkw)],
        out_specs=[row(w) for w, _ in outs],
        out_shape=[jax.ShapeDtypeStruct((n, w), t) for w, t in outs],
        compiler_params=_params(("parallel",)),
        name="attn_proj",
    )(x, wq, wqi, wkv, wkw)


def _gla_proj_kernel(x_ref, wgq_ref, wgk_ref, wgv_ref, wgr_ref, wlr_ref, wa2_ref, ba_ref,
                     gq_ref, gk_ref, gv_ref, la_ref, gr_ref):
    xb = x_ref[...].astype(BF16)
    gq_ref[...] = _dot(xb, wgq_ref[...]) * (GLA_DK ** -0.5)
    gk_ref[...] = _dot(xb, wgk_ref[...])
    gv_ref[...] = _dot(xb, wgv_ref[...]).astype(BF16)
    gr_ref[...] = _dot(xb, wgr_ref[...])
    lr = _dot(xb, wlr_ref[...])
    z = _dot(lr.astype(BF16), wa2_ref[...]) + ba_ref[...]
    la_ref[...] = (jnp.minimum(z, 0.0) - jnp.log(1.0 + jnp.exp(-jnp.abs(z)))) * (1.0 / GLA_NORMALIZER)


def _gla_proj(x, wgq, wgk, wgv, wgr, wlr, wa2, ba, tm=256):
    n, d = x.shape
    tm = min(tm, n)
    assert n % tm == 0
    row = lambda w: pl.BlockSpec((tm, w), lambda i: (i, 0))
    full = lambda w: pl.BlockSpec(w.shape, lambda i: (0, 0))
    kw, vw = wgq.shape[1], wgv.shape[1]
    outs = [(kw, F32), (kw, F32), (vw, BF16), (kw, F32), (vw, F32)]
    return pl.pallas_call(
        _gla_proj_kernel,
        grid=(n // tm,),
        in_specs=[row(d), full(wgq), full(wgk), full(wgv), full(wgr), full(wlr), full(wa2), full(ba)],
        out_specs=[row(w) for w, _ in outs],
        out_shape=[jax.ShapeDtypeStruct((n, w), t) for w, t in outs],
        compiler_params=_params(("parallel",)),
        name="gla_proj",
    )(x, wgq, wgk, wgv, wgr, wlr, wa2, ba)


def _sortable_key(a):
    bits = lax.bitcast_convert_type(a + 0.0, I32)
    return bits ^ ((bits >> 31) & 0x7FFFFFFF)


def _dsa_kernel(q_ref, qi_ref, wi_ref, k_ref, v_ref, ki2_ref, bw_ref, o_ref,
                sc_ref, sc16_ref, wb_ref, qip_ref, thr_ref, qg_ref, acc_ref, m_ref, lg_ref, corr_ref,
                *, tq, cbs, cbf, qb0, n_sel):
    qb = pl.program_id(1) + qb0
    q0 = qb * LANES
    rows = GROUP * tq

    wi = wi_ref[0]
    for h in range(IDX_HEADS):
        wb_ref[h] = jnp.broadcast_to(wi[:, h:h + 1], (tq, LANES))
    qi = qi_ref[0]
    row = lax.broadcasted_iota(I32, (tq, LANES), 0)
    lane = lax.broadcasted_iota(I32, (tq, LANES), 1)
    for p in range(IDX_HEADS // 2):
        pair = qi[:, p * LANES:(p + 1) * LANES]
        qip_ref[0, p * tq:(p + 1) * tq, :] = jnp.where(lane < IDX_DIM, pair, jnp.zeros_like(pair))
        qip_ref[1, p * tq:(p + 1) * tq, :] = jnp.where(lane >= IDX_DIM, pair, jnp.zeros_like(pair))
    limit = q0 + (row // CHUNK + 1) * CHUNK
    tks = cbs * LANES

    def score_tile(kt, carry):
        start = pl.multiple_of(kt * tks, tks)
        ki2 = ki2_ref[0, pl.ds(start, tks), :]
        accs = [jnp.zeros((tq, LANES), F32) for _ in range(cbs)]
        for half in range(2):
            s = _dot_nt(qip_ref[half], ki2)
            for c in range(cbs):
                a = accs[c]
                for p in range(IDX_HEADS // 2):
                    sp = s[p * tq:(p + 1) * tq, c * LANES:(c + 1) * LANES]
                    a = a + jnp.maximum(sp, 0.0) * wb_ref[2 * p + half]
                accs[c] = a
        for c in range(cbs):
            adm = (kt * cbs + c) * LANES + lane < limit
            bits = lax.bitcast_convert_type(accs[c] + 0.0, I32)
            sc_ref[kt * cbs + c] = jnp.where(adm, bits ^ ((bits >> 31) & 0x7FFFFFFF), INT_MIN)
            top = lax.bitcast_convert_type(bits & -65536, F32)
            sc16_ref[kt * cbs + c] = jnp.where(adm, top, -jnp.inf).astype(BF16)
        return carry

    n_scored = _round_up(jnp.maximum(qb + 1, cbf), max(cbs, COUNT_UNROLL))
    lax.fori_loop(0, n_scored // cbs, score_tile, 0)

    one16 = jnp.ones((tq, LANES), BF16)
    zero16 = jnp.zeros((tq, LANES), BF16)

    def search(keys_ref, keys16_ref, n_groups):
        def upper_bit(i, state):
            tu, cnt_t = state
            cand_u = tu | lax.shift_left(jnp.int32(1), 15 - i)
            k32 = (jnp.clip(cand_u, BF16_NAN_CODES, 2 ** 16 - BF16_NAN_CODES) - 2 ** 15) << 16
            cval = lax.bitcast_convert_type(k32 ^ ((k32 >> 31) & 0x7FFF0000), F32)
            cand_b = jnp.broadcast_to(cval, (tq, LANES)).astype(BF16)

            def count(gi, acc):
                for u in range(COUNT_UNROLL):
                    acc = acc + jnp.where(keys16_ref[gi * COUNT_UNROLL + u] >= cand_b, one16, zero16)
                return acc

            acc = lax.fori_loop(0, n_groups, count, zero16)
            cnt = jnp.sum(acc.astype(F32), axis=1, keepdims=True)
            ok = cnt >= n_sel
            return jnp.where(ok, cand_u, tu), jnp.where(ok, cnt, cnt_t)

        tu, cnt_t = lax.fori_loop(0, 16, upper_bit,
                                  (jnp.zeros((tq, 1), I32), jnp.full((tq, 1), float(2 ** 30), F32)))

        def lower_bit(state):
            i, t, cnt_t = state
            cand = t | lax.shift_left(jnp.int32(1), 15 - i)
            cand_b = jnp.broadcast_to(cand, (tq, LANES))

            def count(gi, acc):
                for u in range(COUNT_UNROLL):
                    acc = acc + jnp.where(keys_ref[gi * COUNT_UNROLL + u] >= cand_b, 1.0, 0.0)
                return acc

            acc = lax.fori_loop(0, n_groups, count, jnp.zeros((tq, LANES), F32))
            cnt = jnp.sum(acc, axis=1, keepdims=True)
            ok = cnt >= n_sel
            return i + 1, jnp.where(ok, cand, t), jnp.where(ok, cnt, cnt_t)

        def undecided(state):
            i, _, cnt_t = state
            return (i < 16) & (jnp.max(jnp.abs(cnt_t - n_sel)) > 0.0)

        _, t, cnt_t = lax.while_loop(undecided, lower_bit, (jnp.int32(0), (tu - 2 ** 15) << 16, cnt_t))
        return t, cnt_t

    t, cnt_t = search(sc_ref, sc16_ref, _round_up(qb + 1, COUNT_UNROLL) // COUNT_UNROLL)
    thr_ref[...] = jnp.broadcast_to(jnp.maximum(t, INT_MIN + 1), (tq, LANES))

    tied = (cnt_t > n_sel) & (cnt_t < float(2 ** 29))

    @pl.when(jnp.max(jnp.where(tied, 1.0, 0.0)) > 0.0)
    def _():
        t_b = jnp.broadcast_to(t, (tq, LANES))
        tied_b = jnp.broadcast_to(jnp.where(tied, 1, 0), (tq, LANES)) > 0

        def count_where(pred):
            def body(tt, acc):
                return acc + jnp.where(pred(sc_ref[tt], tt * LANES + lane), 1.0, 0.0)
            acc = lax.fori_loop(0, qb + 1, body, jnp.zeros((tq, LANES), F32))
            return jnp.sum(acc, axis=1, keepdims=True)

        keep = n_sel - count_where(lambda key, kpos: key > t_b)
        n_bits = max(1, (sc_ref.shape[0] * LANES - 1).bit_length())

        def position_bit(i, cut):
            cand = cut | lax.shift_left(jnp.int32(1), n_bits - 1 - i)
            cand_b = jnp.broadcast_to(cand, (tq, LANES))
            cnt = count_where(lambda key, kpos: (key == t_b) & (kpos < cand_b))
            return jnp.where(cnt <= keep, cand, cut)

        cut = lax.fori_loop(0, n_bits, position_bit, jnp.zeros((tq, 1), I32))
        cut_b = jnp.broadcast_to(cut, (tq, LANES))

        def demote(tt, carry):
            key = sc_ref[tt]
            late = tied_b & (key == t_b) & (tt * LANES + lane >= cut_b)
            sc_ref[tt] = jnp.where(late, key - 1, key)
            return carry

        lax.fori_loop(0, qb + 1, demote, 0)

    q = q_ref[0]
    thr = thr_ref[...]
    vw = 2 * HEAD_DIM

    def values(span, kh):
        vt = v_ref[0, span, kh * HEAD_DIM:(kh + 1) * HEAD_DIM]
        one_col = lax.broadcasted_iota(I32, vt.shape, 1) == 0
        return jnp.concatenate([vt, jnp.where(one_col, 1.0, 0.0).astype(BF16)], axis=1)

    def flash_update(kh, logits, vt):
        m_prev = m_ref[kh]
        m_new = jnp.maximum(m_prev, jnp.max(logits, axis=1, keepdims=True))
        p = jnp.exp2(logits - m_new)
        acc_ref[kh] = jnp.exp2(m_prev - m_new) * acc_ref[kh] + _dot(p.astype(BF16), vt)
        m_ref[kh] = m_new

    for kh in range(N_KV_HEADS):
        for g in range(GROUP):
            c0 = (kh * GROUP + g) * HEAD_DIM
            qg_ref[kh, g * tq:(g + 1) * tq, :] = q[:, c0:c0 + HEAD_DIM]
        m_ref[kh] = jnp.full((rows, 1), NEG_BIG, F32)
        acc_ref[kh] = jnp.zeros((rows, vw), F32)

    def drop(tt, extra=0.0):
        return jnp.concatenate([jnp.where(sc_ref[tt] >= thr, 0.0, NEG_BIG) + extra] * GROUP, axis=0)

    def attend(start, width, dropped, bias):
        for kh in range(N_KV_HEADS):
            kt_h = k_ref[0, pl.ds(start, width), kh * HEAD_DIM:(kh + 1) * HEAD_DIM]
            logits = _dot_nt(qg_ref[kh], kt_h) + dropped
            if bias is not None:
                logits = logits + bias[kh]
            flash_update(kh, logits, values(pl.ds(start, width), kh))

    n_free = jnp.maximum(qb - 1, 0)

    n_far = jnp.maximum((n_free + cbf - 1) // cbf, 1)
    wf = cbf * LANES

    def far_span(j):
        t_hi = n_free - j * cbf
        t_lo = jnp.maximum(t_hi - cbf, 0)
        return t_lo, t_hi, pl.ds(pl.multiple_of(t_lo * LANES, LANES), wf)

    def far_logits(j):
        t_lo, t_hi, span = far_span(j)
        parts = [drop(t_lo + c, jnp.where(t_lo + c < t_hi, 0.0, NEG_BIG)) for c in range(cbf)]
        dropped = jnp.concatenate(parts, axis=1)
        for kh in range(N_KV_HEADS):
            logits = _dot_nt(qg_ref[kh], k_ref[0, span, kh * HEAD_DIM:(kh + 1) * HEAD_DIM]) + dropped
            m_prev = m_ref[kh]
            m_new = jnp.maximum(m_prev, jnp.max(logits, axis=1, keepdims=True))
            lg_ref[j % 2, kh] = logits
            corr_ref[j % 2, kh, 0] = jnp.exp2(m_prev - m_new)
            corr_ref[j % 2, kh, 1] = m_new
            m_ref[kh] = m_new

    def far_values(j):
        _, _, span = far_span(j)
        for kh in range(N_KV_HEADS):
            p = jnp.exp2(lg_ref[j % 2, kh] - corr_ref[j % 2, kh, 1]).astype(BF16)
            acc_ref[kh] = corr_ref[j % 2, kh, 0] * acc_ref[kh] + _dot(p, values(span, kh))

    t0 = jnp.maximum(qb - 1, 0)
    variant = jnp.where(qb == 0, 1, 0)
    attend(pl.multiple_of(t0 * LANES, LANES), 2 * LANES, jnp.concatenate([drop(t0), drop(t0 + 1)], axis=1),
           [bw_ref[variant, kh] for kh in range(N_KV_HEADS)])
    far_logits(0)

    def far_step(j, carry):
        far_values(j)
        far_logits(j + 1)
        return carry

    lax.fori_loop(0, n_far - 1, far_step, 0)
    far_values(n_far - 1)

    for kh in range(N_KV_HEADS):
        acc = acc_ref[kh]
        o = acc[:, :HEAD_DIM] / acc[:, HEAD_DIM:HEAD_DIM + 1]
        for g in range(GROUP):
            c0 = (kh * GROUP + g) * HEAD_DIM
            o_ref[0, :, c0:c0 + HEAD_DIM] = o[g * tq:(g + 1) * tq].astype(o_ref.dtype)


def _dsa(q, qi, wi, kb, vb, ki2, bw, *, tq, cbs, cbf, qb0, n_sel):
    bsz, t, aw = q.shape
    length = kb.shape[1]
    assert t % tq == 0 and length % (LANES * max(cbs, COUNT_UNROLL)) == 0 and length >= cbf * LANES
    n_tiles = length // LANES
    rows = GROUP * tq
    qspec = lambda w: pl.BlockSpec((1, tq, w), lambda b, i: (b, i, 0))
    kspec = lambda w: pl.BlockSpec((1, length, w), lambda b, i: (b, 0, 0), pipeline_mode=pl.Buffered(1))
    return pl.pallas_call(
        functools.partial(_dsa_kernel, tq=tq, cbs=cbs, cbf=cbf, qb0=qb0, n_sel=n_sel),
        grid=(bsz, t // tq),
        in_specs=[
            qspec(aw), qspec(qi.shape[2]), qspec(wi.shape[2]),
            kspec(kb.shape[2]), kspec(vb.shape[2]), kspec(ki2.shape[2]),
            pl.BlockSpec(bw.shape, lambda b, i: (0, 0, 0, 0)),
        ],
        out_specs=qspec(aw),
        out_shape=jax.ShapeDtypeStruct((bsz, t, aw), BF16),
        scratch_shapes=[
            pltpu.VMEM((n_tiles, tq, LANES), I32),
            pltpu.VMEM((n_tiles, tq, LANES), BF16),
            pltpu.VMEM((IDX_HEADS, tq, LANES), F32),
            pltpu.VMEM((2, IDX_HEADS // 2 * tq, LANES), BF16),
            pltpu.VMEM((tq, LANES), I32),
            pltpu.VMEM((N_KV_HEADS, rows, HEAD_DIM), BF16),
            pltpu.VMEM((N_KV_HEADS, rows, 2 * HEAD_DIM), F32),
            pltpu.VMEM((N_KV_HEADS, rows, 1), F32),
            pltpu.VMEM((2, N_KV_HEADS, rows, cbf * LANES), F32),
            pltpu.VMEM((2, N_KV_HEADS, 2, rows, 1), F32),
        ],
        compiler_params=_params(("parallel", "arbitrary")),
        name="dsa",
    )(q, qi, wi, kb, vb, ki2, bw)


def _rel_bucket(rel):
    half = N_BUCKETS // 2
    max_exact = half // 2
    ret = jnp.where(rel > 0, half, 0)
    n = jnp.abs(rel)
    nf = jnp.maximum(n, 1).astype(F32)
    large = max_exact + (jnp.log(nf / max_exact) / math.log(MAX_DISTANCE / max_exact)
                         * (half - max_exact)).astype(I32)
    large = jnp.minimum(large, half - 1)
    return ret + jnp.where(n < max_exact, n, large)


def _bias_window(rel_bias, tq):
    i = jnp.arange(tq, dtype=I32)[:, None]
    c = jnp.arange(2 * LANES, dtype=I32)[None, :]
    rel = c - LANES - i
    far = rel_bias[N_BUCKETS // 2 - 1]
    onehot = (_rel_bucket(rel)[:, :, None] == jnp.arange(N_BUCKETS, dtype=I32)).astype(F32)
    looked_up = jnp.einsum("rcb,bh->rch", onehot, rel_bias.astype(F32), precision=lax.Precision.HIGHEST)
    bias = (looked_up - far) * LOG2E
    bias = jnp.where((rel <= -MAX_DISTANCE)[:, :, None], 0.0, bias)
    bias = bias.transpose(2, 0, 1).reshape(N_KV_HEADS, GROUP * tq, 2 * LANES)
    first = jnp.concatenate([bias[..., LANES:], jnp.zeros_like(bias[..., LANES:])], axis=-1)
    return jnp.stack([bias, first]).astype(F32)


def _split3(x):
    hi = x.astype(BF16)
    r = x - hi.astype(F32)
    mid = r.astype(BF16)
    lo = (r - mid.astype(F32)).astype(BF16)
    return hi, mid, lo


def _gla_kernel(gq_ref, gk_ref, gv_ref, la_ref, gr_ref, g_ref, s0_ref, o_ref, sf_ref, st_ref, a_ref, b_ref,
                *, chunk, n_chunks):
    step = pl.program_id(1)

    @pl.when(step == 0)
    def _():
        for h in range(GLA_HEADS):
            st_ref[h] = s0_ref[0, h].astype(F32).T

    ri = lax.broadcasted_iota(I32, (chunk, chunk), 0)
    ci = lax.broadcasted_iota(I32, (chunk, chunk), 1)
    tri = jnp.where(ci <= ri, 1.0, 0.0).astype(BF16)
    rk = lax.broadcasted_iota(I32, (chunk, GLA_DK), 0)
    levels = []
    size = chunk // 2
    while size >= GLA_SUB:
        levels.append(size)
        size //= 2

    def ref_rows(b, idx):
        out = []
        start = 0
        while start < chunk:
            end = start
            while end < chunk and idx[end] == idx[start]:
                end += 1
            out.append(jnp.broadcast_to(b[idx[start]:idx[start] + 1, :], (end - start, b.shape[1])))
            start = end
        return jnp.concatenate(out, axis=0)

    def row_of(rows8, j):
        pick = lax.broadcasted_iota(I32, rows8.shape, 0) == j % SUBLANES
        return jnp.sum(jnp.where(pick, rows8, 0.0), axis=0, keepdims=True)

    def chunk_heads(rs, exact_diagonal):
        for h in range(GLA_HEADS):
            ks = slice(h * GLA_DK, (h + 1) * GLA_DK)
            vs = slice(h * GLA_DV, (h + 1) * GLA_DV)
            qc = gq_ref[0, rs, ks]
            kc = gk_ref[0, rs, ks]
            vc = gv_ref[0, rs, vs]
            hi, mid, lo = _split3(la_ref[0, rs, ks])
            b = _dot(tri, hi) + _dot(tri, mid) + _dot(tri, lo)
            st_prev = st_ref[h]

            keep = (ri // GLA_SUB == ci // GLA_SUB) & (ci <= ri)
            if exact_diagonal:
                a_ref[...] = jnp.zeros_like(a_ref)
                b_ref[...] = b

                def column(j, carry):
                    grp = pl.multiple_of(j // SUBLANES * SUBLANES, SUBLANES)
                    bj = row_of(b_ref[pl.ds(grp, SUBLANES), :], j)
                    kj = row_of(gk_ref[0, pl.ds(rs.start + grp, SUBLANES), ks], j)
                    pair = qc * kj * jnp.exp(jnp.minimum(b - bj, 0.0))
                    col = jnp.sum(pair, axis=1, keepdims=True)
                    a_ref[...] = jnp.where(keep & (ci == j), col, a_ref[...])
                    return carry

                lax.fori_loop(0, chunk, column, 0)
                a = a_ref[...]
            else:
                bref = ref_rows(b, [(r // GLA_SUB) * GLA_SUB for r in range(chunk)])
                qd = (qc * jnp.exp(b - bref)).astype(BF16)
                kd = (kc * jnp.exp(bref - b)).astype(BF16)
                a = jnp.where(keep, _dot_nt(qd, kd), 0.0)
            for size in levels:
                bref = ref_rows(b, [(r // (2 * size)) * 2 * size + size - 1 for r in range(chunk)])
                upper = (rk // size) % 2 == 1
                ql = (qc * jnp.exp(jnp.where(upper, b - bref, 0.0))).astype(BF16)
                kl = (kc * jnp.exp(jnp.where(upper, 0.0, bref - b))).astype(BF16)
                split = (ri // (2 * size) == ci // (2 * size)) & ((ri // size) % 2 == 1) & ((ci // size) % 2 == 0)
                a = jnp.where(split, _dot_nt(ql, kl), a)

            o = _dot_nt((qc * jnp.exp(b)).astype(BF16), st_prev.astype(BF16)) + _dot(a.astype(BF16), vc)
            b_last = b[chunk - 1:chunk, :]
            kdec = (kc * jnp.exp(b_last - b)).astype(BF16)
            st_ref[h] = jnp.exp(b_last) * st_prev + lax.dot_general(
                vc, kdec, (((0,), (0,)), ((), ())), preferred_element_type=F32)

            mu = jnp.mean(o, axis=-1, keepdims=True)
            d = o - mu
            var = jnp.mean(d * d, axis=-1, keepdims=True)
            gr = gr_ref[0, rs, vs]
            y = d * lax.rsqrt(var + LN_EPS) * g_ref[h] * (gr * jax.nn.sigmoid(gr))
            o_ref[0, rs, vs] = y.astype(o_ref.dtype)

    n_blocks = _round_up(chunk * n_chunks // GLA_SUB, 2 * SUBLANES)
    bi = lax.broadcasted_iota(I32, (n_blocks, chunk * n_chunks), 0)
    rj = lax.broadcasted_iota(I32, (n_blocks, chunk * n_chunks), 1)
    in_block = jnp.where((rj // GLA_SUB == bi) & (rj % GLA_SUB != 0), 1.0, 0.0).astype(BF16)
    steep = jnp.max(_dot(in_block, (-la_ref[0]).astype(BF16))) > GLA_SAFE_EXP

    def chunks(exact_diagonal):
        def chunk_step(c, carry):
            chunk_heads(pl.ds(pl.multiple_of(c * chunk, chunk), chunk), exact_diagonal)
            return carry

        lax.fori_loop(0, n_chunks, chunk_step, 0)

    @pl.when(jnp.logical_not(steep))
    def _():
        chunks(False)

    @pl.when(steep)
    def _():
        chunks(True)

    @pl.when(step == pl.num_programs(1) - 1)
    def _():
        for h in range(GLA_HEADS):
            sf_ref[0, h] = st_ref[h].T.astype(sf_ref.dtype)


def _gla(gq, gk, gv, la, gr, g, s0, *, chunk, n_chunks):
    bsz, t, kw = gq.shape
    vw = gv.shape[2]
    tb = chunk * n_chunks
    assert t % tb == 0
    tok = lambda w: pl.BlockSpec((1, tb, w), lambda b, i: (b, i, 0))
    sspec = pl.BlockSpec((1,) + s0.shape[1:], lambda b, i: (b, 0, 0, 0))
    return pl.pallas_call(
        functools.partial(_gla_kernel, chunk=chunk, n_chunks=n_chunks),
        grid=(bsz, t // tb),
        in_specs=[tok(kw), tok(kw), tok(vw), tok(kw), tok(vw),
                  pl.BlockSpec(g.shape, lambda b, i: (0, 0, 0)), sspec],
        out_specs=[tok(vw), sspec],
        out_shape=[jax.ShapeDtypeStruct((bsz, t, vw), BF16), jax.ShapeDtypeStruct(s0.shape, s0.dtype)],
        scratch_shapes=[pltpu.VMEM((s0.shape[1], s0.shape[3], s0.shape[2]), F32),
                        pltpu.VMEM((chunk, chunk), F32), pltpu.VMEM((chunk, GLA_DK), F32)],
        compiler_params=_params(("parallel", "arbitrary")),
        name="gla",
    )(gq, gk, gv, la, gr, g, s0)


def _out_ln_kernel(x_ref, a_ref, gl_ref, woa_ref, wog_ref, g_ref, b_ref, o_ref, *, alpha):
    mix = _dot(a_ref[...], woa_ref[...]) + _dot(gl_ref[...], wog_ref[...])
    o_ref[...] = _post_norm(x_ref[...], mix, g_ref[...], b_ref[...], alpha)


def _out_ln(x, attn, gla, woa, wog, g, b, alpha, tm=512):
    n, d = x.shape
    tm = min(tm, n)
    assert n % tm == 0
    row = lambda w: pl.BlockSpec((tm, w), lambda i: (i, 0))
    full = lambda w: pl.BlockSpec(w.shape, lambda i: (0, 0))
    return pl.pallas_call(
        functools.partial(_out_ln_kernel, alpha=alpha),
        grid=(n // tm,),
        in_specs=[row(d), row(attn.shape[1]), row(gla.shape[1]), full(woa), full(wog), full(g), full(b)],
        out_specs=row(d),
        out_shape=jax.ShapeDtypeStruct((n, d), F32),
        compiler_params=_params(("parallel",)),
        name="out_ln",
    )(x, attn, gla, woa, wog, g, b)


def _matmul_kernel(x_ref, w_ref, o_ref):
    o_ref[...] = _dot(x_ref[...].astype(BF16), w_ref[...]).astype(o_ref.dtype)


def _matmul(x, w, tm=256, tn=512):
    n, d = x.shape
    m = w.shape[1]
    tm, tn = min(tm, n), min(tn, m)
    assert n % tm == 0 and m % tn == 0
    return pl.pallas_call(
        _matmul_kernel,
        grid=(n // tm, m // tn),
        in_specs=[pl.BlockSpec((tm, d), lambda i, j: (i, 0)), pl.BlockSpec((d, tn), lambda i, j: (0, j))],
        out_specs=pl.BlockSpec((tm, tn), lambda i, j: (i, j)),
        out_shape=jax.ShapeDtypeStruct((n, m), F32),
        compiler_params=_params(("parallel", "parallel")),
        name="matmul",
    )(x, w)


def _mem_ln_kernel(x_ref, wq_ref, mk_ref, mv_ref, wo_ref, g_ref, b_ref, o_ref, *, alpha):
    x = x_ref[0]
    q = _dot(x.astype(BF16), wq_ref[...])
    scale = MEM_HEAD_DIM ** -0.5
    heads = []
    for h in range(MEM_HEADS):
        hs = slice(h * MEM_HEAD_DIM, (h + 1) * MEM_HEAD_DIM)
        logits = _dot_nt(q[:, hs].astype(BF16), mk_ref[0, :, hs]) * scale
        m = jnp.max(logits, axis=-1, keepdims=True)
        p = jnp.exp(logits - m)
        p = p / jnp.sum(p, axis=-1, keepdims=True)
        heads.append(_dot(p.astype(BF16), mv_ref[0, :, hs]))
    o = jnp.concatenate(heads, axis=1).astype(BF16)
    o_ref[0] = _post_norm(x, _dot(o, wo_ref[...]), g_ref[...], b_ref[...], alpha)


def _mem_ln(x, wq, mk, mv, wo, g, b, alpha, tm=512):
    bsz, t, d = x.shape
    tm = min(tm, t)
    assert t % tm == 0
    full = lambda w: pl.BlockSpec(w.shape, lambda bb, i: (0, 0))
    mem = pl.BlockSpec((1,) + mk.shape[1:], lambda bb, i: (bb, 0, 0))
    tok = pl.BlockSpec((1, tm, d), lambda bb, i: (bb, i, 0))
    return pl.pallas_call(
        functools.partial(_mem_ln_kernel, alpha=alpha),
        grid=(bsz, t // tm),
        in_specs=[tok, full(wq), mem, mem, full(wo), full(g), full(b)],
        out_specs=tok,
        out_shape=jax.ShapeDtypeStruct((bsz, t, d), F32),
        compiler_params=_params(("parallel", "parallel")),
        name="mem_ln",
    )(x, wq, mk, mv, wo, g, b)


def _split_w_in(w_in, w_a2, b_a):
    attn_w = N_HEADS * HEAD_DIM
    kv_w = N_KV_HEADS * HEAD_DIM
    gla_kw = GLA_HEADS * GLA_DK
    gla_vw = GLA_HEADS * GLA_DV
    widths = (attn_w, kv_w, kv_w, IDX_HEADS * IDX_DIM, IDX_DIM, IDX_HEADS, gla_kw, gla_kw, gla_vw, GLA_RANK, gla_vw)
    assert sum(widths) == w_in.shape[1]
    offs = [0]
    for w in widths:
        offs.append(offs[-1] + w)
    col = lambda a, b: w_in[:, offs[a]:offs[b]].astype(BF16)
    wq, wkv, wqi = col(0, 1), col(1, 3), col(3, 4)
    wkw = jnp.pad(col(4, 6), ((0, 0), (0, LANES - IDX_DIM - IDX_HEADS)))
    wgq, wgk, wgv, wgr = col(6, 7), col(7, 8), col(8, 9), col(10, 11)
    wlr = jnp.pad(col(9, 10), ((0, 0), (0, LANES - GLA_RANK)))
    wa2 = jnp.pad(w_a2.astype(BF16), ((0, LANES - GLA_RANK), (0, 0)))
    return (wq, wqi, wkv, wkw), (wgq, wgk, wgv, wgr, wlr, wa2, b_a.reshape(1, -1).astype(F32))


def kernel(x_prompt, x_sample, cache_k, cache_v, cache_idx_k, state_gla, cache_mem_k, cache_mem_v, mem_prompt,
           rel_bias, ln_g, ln_b, ffn1_wg, ffn1_wu, ffn1_wd, w_in, w_a2, b_a, gla_norm_g, w_o, w_mq, w_mk, w_mv,
           w_mo, ffn2_wg, ffn2_wu, ffn2_wd):
    depth = w_in.shape[0]
    bp, tp, d = x_prompt.shape
    bs, ts, _ = x_sample.shape
    past = cache_k.shape[2]
    alpha = (2.0 * depth) ** 0.25
    attn_w = N_HEADS * HEAD_DIM
    kv_w = N_KV_HEADS * HEAD_DIM
    mem_w = MEM_HEADS * MEM_HEAD_DIM
    assert tp % LANES == 0 and ts == CHUNK and past % LANES == 0

    xp = x_prompt.reshape(bp * tp, d)
    xs = x_sample.reshape(bs * ts, d)
    pk, pv, pki, pS, pmk, pmv = [], [], [], [], [], []
    sk, sv, ski, sS = [], [], [], []
    bw_p = _bias_window(rel_bias, LANES)
    bw_s = bw_p.reshape(2, N_KV_HEADS, GROUP, LANES, 2 * LANES)[:, :, :, :ts].reshape(2, N_KV_HEADS, GROUP * ts, 2 * LANES)
    for l in range(depth):
        g = lambda i: ln_g[l, i].reshape(1, d)
        b = lambda i: ln_b[l, i].reshape(1, d)
        bf = lambda w: w[l].astype(BF16)
        f1 = (bf(ffn1_wg), bf(ffn1_wu), bf(ffn1_wd))
        f2 = (bf(ffn2_wg), bf(ffn2_wu), bf(ffn2_wd))
        w_attn, w_gla = _split_w_in(w_in[l], w_a2[l], b_a[l])
        woa, wog = w_o[l, :attn_w].astype(BF16), w_o[l, attn_w:].astype(BF16)
        gn = gla_norm_g[l].reshape(GLA_HEADS, 1, GLA_DV).astype(F32)

        xp = _ffn_ln(xp, *f1, g(0), b(0), alpha)
        xs = _ffn_ln(xs, *f1, g(0), b(0), alpha)

        q, qi, k, v, kb, vb, ki, ki2, wi = _attn_proj(xp, *w_attn)
        r3 = lambda a, n=bp, t=tp: a.reshape(n, t, a.shape[-1])
        attn = _dsa(r3(q), r3(qi), r3(wi), r3(kb), r3(vb), r3(ki2), bw_p,
                    tq=LANES, cbs=DSA_SCORE_TILES, cbf=DSA_FAR_TILES, qb0=0, n_sel=min(TOPK_MAX, tp // 4))
        gq, gk, gv, la, gr = _gla_proj(xp, *w_gla)
        s0 = jnp.zeros((bp, GLA_HEADS, GLA_DK, GLA_DV), state_gla.dtype)
        go, s_p = _gla(r3(gq), r3(gk), r3(gv), r3(la), r3(gr), gn, s0, chunk=CHUNK, n_chunks=4)
        xp = _out_ln(xp, attn.reshape(bp * tp, attn_w), go.reshape(bp * tp, -1), woa, wog, g(1), b(1), alpha)
        pk.append(k.reshape(bp, tp, N_KV_HEADS, HEAD_DIM))
        pv.append(v.reshape(bp, tp, N_KV_HEADS, HEAD_DIM))
        pki.append(ki.reshape(bp, tp, IDX_DIM))
        pS.append(s_p)

        q, qi, k, v, kb, vb, ki, ki2, wi = _attn_proj(xs, *w_attn)
        r3s = lambda a: a.reshape(bs, ts, a.shape[-1])
        pad_t = (-(past + ts)) % (LANES * max(DSA_SCORE_TILES, COUNT_UNROLL))
        cat = lambda c, n: jnp.pad(jnp.concatenate([c.astype(BF16), r3s(n)], axis=1), ((0, 0), (0, pad_t), (0, 0)))
        k_all = cat(cache_k[l].reshape(bs, past, kv_w), kb)
        v_all = cat(cache_v[l].reshape(bs, past, kv_w), vb)
        ki2_all = cat(jnp.concatenate([cache_idx_k[l]] * 2, axis=-1), ki2)
        attn = _dsa(r3s(q), r3s(qi), r3s(wi), k_all, v_all, ki2_all, bw_s,
                    tq=ts, cbs=DSA_SCORE_TILES, cbf=DSA_FAR_TILES, qb0=past // LANES,
                    n_sel=min(TOPK_MAX, (past + ts) // 4))
        gq, gk, gv, la, gr = _gla_proj(xs, *w_gla)
        go, s_s = _gla(r3s(gq), r3s(gk), r3s(gv), r3s(la), r3s(gr), gn, state_gla[l], chunk=ts, n_chunks=1)
        xs = _out_ln(xs, attn.reshape(bs * ts, attn_w), go.reshape(bs * ts, -1), woa, wog, g(1), b(1), alpha)
        sk.append(k.reshape(bs, ts, N_KV_HEADS, HEAD_DIM))
        sv.append(v.reshape(bs, ts, N_KV_HEADS, HEAD_DIM))
        ski.append(ki.reshape(bs, ts, IDX_DIM))
        sS.append(s_s)

        n_mem = mem_prompt.shape[1]
        mkv = _matmul(mem_prompt.reshape(bp * n_mem, d), jnp.concatenate([bf(w_mk), bf(w_mv)], axis=1))
        mk_p = mkv[:, :mem_w].reshape(bp, n_mem, mem_w)
        mv_p = mkv[:, mem_w:].reshape(bp, n_mem, mem_w)
        wmq, wmo = bf(w_mq), bf(w_mo)
        xp = _mem_ln(xp.reshape(bp, tp, d), wmq, mk_p.astype(BF16), mv_p.astype(BF16), wmo, g(2), b(2),
                     alpha).reshape(bp * tp, d)
        xs = _mem_ln(xs.reshape(bs, ts, d), wmq, cache_mem_k[l].reshape(bs, n_mem, mem_w).astype(BF16),
                     cache_mem_v[l].reshape(bs, n_mem, mem_w).astype(BF16), wmo, g(2), b(2),
                     alpha).reshape(bs * ts, d)
        pmk.append(mk_p.reshape(bp, n_mem, MEM_HEADS, MEM_HEAD_DIM))
        pmv.append(mv_p.reshape(bp, n_mem, MEM_HEADS, MEM_HEAD_DIM))

        xp = _ffn_ln(xp, *f2, g(3), b(3), alpha)
        xs = _ffn_ln(xs, *f2, g(3), b(3), alpha)

    return (xp.reshape(bp, tp, d), xs.reshape(bs, ts, d),
            jnp.stack(pk), jnp.stack(pv), jnp.stack(pki), jnp.stack(pS), jnp.stack(pmk), jnp.stack(pmv),
            jnp.stack(sk), jnp.stack(sv), jnp.stack(ski), jnp.stack(sS))
```

```python
import functools
import math

import jax
import jax.numpy as jnp
from jax import lax
from jax.experimental import pallas as pl
from jax.experimental.pallas import tpu as pltpu

F32 = jnp.float32
BF16 = jnp.bfloat16
I32 = jnp.int32
I16 = jnp.int16

LANES = 128
SUBLANES = 8
VMEM_BYTES_V7X = 64 * 1024 * 1024
VMEM_LIMIT = VMEM_BYTES_V7X - 4 * 1024 * 1024

CHUNK = 64
N_HEADS = 8
N_KV_HEADS = 2
HEAD_DIM = 128
GROUP = N_HEADS // N_KV_HEADS
IDX_HEADS = 16
IDX_DIM = 64
TOPK_MAX = 256
GLA_HEADS = 4
GLA_DK = 128
GLA_DV = 256
GLA_RANK = 16
GLA_NORMALIZER = 16.0
MEM_HEADS = 4
MEM_HEAD_DIM = 128
N_BUCKETS = 32
MAX_DISTANCE = 128
LN_EPS = 1e-5
INT_MIN = -2 ** 31
I16_MIN = -2 ** 15
NEG_BIG = -1e30
GLA_SUB = 16
GLA_SAFE_EXP = 60.0
COUNT_UNROLL = 8
DSA_SCORE_TILES = 8
DSA_FAR_TILES = 8
LOG2E = math.log2(math.e)


def _round_up(x, m):
    return (x + m - 1) // m * m


def _dot(a, b):
    return jnp.dot(a, b, preferred_element_type=F32)


def _dot_nt(a, b):
    return lax.dot_general(a, b, (((1,), (1,)), ((), ())), preferred_element_type=F32)


def _params(semantics):
    return pltpu.CompilerParams(dimension_semantics=semantics, vmem_limit_bytes=VMEM_LIMIT)


def _post_norm(x, sub, g, b, alpha):
    h = alpha * x + sub
    mu = jnp.mean(h, axis=-1, keepdims=True)
    d = h - mu
    var = jnp.mean(d * d, axis=-1, keepdims=True)
    return d * lax.rsqrt(var + LN_EPS) * g + b


def _ffn_ln_kernel(x_ref, wg_ref, wu_ref, wd_ref, g_ref, b_ref, o_ref, xb_ref, acc_ref, *, alpha):
    j = pl.program_id(1)

    @pl.when(j == 0)
    def _():
        xb_ref[...] = x_ref[...].astype(BF16)
        acc_ref[...] = jnp.zeros_like(acc_ref)

    xb = xb_ref[...]
    hg = _dot(xb, wg_ref[...])
    hu = _dot(xb, wu_ref[...])
    h = (hg * jax.nn.sigmoid(hg)) * hu
    acc_ref[...] += _dot(h.astype(BF16), wd_ref[...])

    @pl.when(j == pl.num_programs(1) - 1)
    def _():
        o_ref[...] = _post_norm(x_ref[...], 0.5 * acc_ref[...], g_ref[...], b_ref[...], alpha)


def _ffn_ln(x, wg, wu, wd, g, b, alpha, tm=512, tf=512):
    n, d = x.shape
    f = wg.shape[1]
    tm = min(tm, n)
    assert n % tm == 0 and f % tf == 0
    return pl.pallas_call(
        functools.partial(_ffn_ln_kernel, alpha=alpha),
        grid=(n // tm, f // tf),
        in_specs=[
            pl.BlockSpec((tm, d), lambda i, j: (i, 0)),
            pl.BlockSpec((d, tf), lambda i, j: (0, j)),
            pl.BlockSpec((d, tf), lambda i, j: (0, j)),
            pl.BlockSpec((tf, d), lambda i, j: (j, 0)),
            pl.BlockSpec((1, d), lambda i, j: (0, 0)),
            pl.BlockSpec((1, d), lambda i, j: (0, 0)),
        ],
        out_specs=pl.BlockSpec((tm, d), lambda i, j: (i, 0)),
        out_shape=jax.ShapeDtypeStruct((n, d), F32),
        scratch_shapes=[pltpu.VMEM((tm, d), BF16), pltpu.VMEM((tm, d), F32)],
        compiler_params=_params(("parallel", "arbitrary")),
        name="ffn_ln",
    )(x, wg, wu, wd, g, b)


def _attn_proj_kernel(x_ref, wq_ref, wqi_ref, wkv_ref, wkw_ref,
                      q_ref, qi_ref, k_ref, v_ref, kb_ref, vb_ref, ki_ref, ki2_ref, wi_ref):
    xb = x_ref[...].astype(BF16)
    q_ref[...] = (_dot(xb, wq_ref[...]) * (HEAD_DIM ** -0.5 * LOG2E)).astype(BF16)
    qi_ref[...] = _dot(xb, wqi_ref[...]).astype(BF16)
    kv = _dot(xb, wkv_ref[...])
    nkv = kv.shape[1] // 2
    k_ref[...] = kv[:, :nkv]
    v_ref[...] = kv[:, nkv:]
    kb_ref[...] = kv[:, :nkv].astype(BF16)
    vb_ref[...] = kv[:, nkv:].astype(BF16)
    kw = _dot(xb, wkw_ref[...])
    ki_ref[...] = kw[:, :IDX_DIM]
    wi_ref[...] = kw[:, IDX_DIM:IDX_DIM + IDX_HEADS]
    lane = lax.broadcasted_iota(I32, kw.shape, 1)
    ki2_ref[...] = jnp.where(lane < IDX_DIM, kw, pltpu.roll(kw, IDX_DIM, 1)).astype(BF16)


def _attn_proj(x, wq, wqi, wkv, wkw, tm=256):
    n, d = x.shape
    tm = min(tm, n)
    assert n % tm == 0
    row = lambda w: pl.BlockSpec((tm, w), lambda i: (i, 0))
    full = lambda w: pl.BlockSpec(w.shape, lambda i: (0, 0))
    nq, nqi, nkv = wq.shape[1], wqi.shape[1], wkv.shape[1] // 2
    outs = [
        (nq, BF16), (nqi, BF16), (nkv, F32), (nkv, F32), (nkv, BF16), (nkv, BF16),
        (IDX_DIM, F32), (LANES, BF16), (IDX_HEADS, F32),
    ]
    return pl.pallas_call(
        _attn_proj_kernel,
        grid=(n // tm,),
        in_specs=[row(d), full(wq), full(wqi), full(wkv), full(wkw)],
        out_specs=[row(w) for w, _ in outs],
        out_shape=[jax.ShapeDtypeStruct((n, w), t) for w, t in outs],
        compiler_params=_params(("parallel",)),
        name="attn_proj",
    )(x, wq, wqi, wkv, wkw)


def _gla_proj_kernel(x_ref, wgq_ref, wgk_ref, wgv_ref, wgr_ref, wlr_ref, wa2_ref, ba_ref,
                     gq_ref, gk_ref, gv_ref, la_ref, gr_ref):
    xb = x_ref[...].astype(BF16)
    gq_ref[...] = _dot(xb, wgq_ref[...]) * (GLA_DK ** -0.5)
    gk_ref[...] = _dot(xb, wgk_ref[...])
    gv_ref[...] = _dot(xb, wgv_ref[...]).astype(BF16)
    gr_ref[...] = _dot(xb, wgr_ref[...])
    lr = _dot(xb, wlr_ref[...])
    z = _dot(lr.astype(BF16), wa2_ref[...]) + ba_ref[...]
    la_ref[...] = (jnp.minimum(z, 0.0) - jnp.log(1.0 + jnp.exp(-jnp.abs(z)))) * (1.0 / GLA_NORMALIZER)


def _gla_proj(x, wgq, wgk, wgv, wgr, wlr, wa2, ba, tm=256):
    n, d = x.shape
    tm = min(tm, n)
    assert n % tm == 0
    row = lambda w: pl.BlockSpec((tm, w), lambda i: (i, 0))
    full = lambda w: pl.BlockSpec(w.shape, lambda i: (0, 0))
    kw, vw = wgq.shape[1], wgv.shape[1]
    outs = [(kw, F32), (kw, F32), (vw, BF16), (kw, F32), (vw, F32)]
    return pl.pallas_call(
        _gla_proj_kernel,
        grid=(n // tm,),
        in_specs=[row(d), full(wgq), full(wgk), full(wgv), full(wgr), full(wlr), full(wa2), full(ba)],
        out_specs=[row(w) for w, _ in outs],
        out_shape=[jax.ShapeDtypeStruct((n, w), t) for w, t in outs],
        compiler_params=_params(("parallel",)),
        name="gla_proj",
    )(x, wgq, wgk, wgv, wgr, wlr, wa2, ba)


def _sortable_key(a):
    bits = lax.bitcast_convert_type(a + 0.0, I32)
    return bits ^ ((bits >> 31) & 0x7FFFFFFF)


def _dsa_kernel(q_ref, qi_ref, wi_ref, k_ref, v_ref, ki2_ref, bw_ref, o_ref,
                sc_ref, hi_ref, lo_ref, wb_ref, qip_ref, thr_ref, qg_ref, acc_ref, m_ref, lg_ref, corr_ref,
                *, tq, cbs, cbf, qb0, n_sel):
    qb = pl.program_id(1) + qb0
    q0 = qb * LANES
    rows = GROUP * tq

    wi = wi_ref[0]
    for h in range(IDX_HEADS):
        wb_ref[h] = jnp.broadcast_to(wi[:, h:h + 1], (tq, LANES))
    qi = qi_ref[0]
    row = lax.broadcasted_iota(I32, (tq, LANES), 0)
    lane = lax.broadcasted_iota(I32, (tq, LANES), 1)
    for p in range(IDX_HEADS // 2):
        pair = qi[:, p * LANES:(p + 1) * LANES]
        qip_ref[0, p * tq:(p + 1) * tq, :] = jnp.where(lane < IDX_DIM, pair, jnp.zeros_like(pair))
        qip_ref[1, p * tq:(p + 1) * tq, :] = jnp.where(lane >= IDX_DIM, pair, jnp.zeros_like(pair))
    limit = q0 + (row // CHUNK + 1) * CHUNK
    tks = cbs * LANES

    def score_tile(kt, carry):
        start = pl.multiple_of(kt * tks, tks)
        ki2 = ki2_ref[0, pl.ds(start, tks), :]
        accs = [jnp.zeros((tq, LANES), F32) for _ in range(cbs)]
        for half in range(2):
            s = _dot_nt(qip_ref[half], ki2)
            for c in range(cbs):
                a = accs[c]
                for p in range(IDX_HEADS // 2):
                    sp = s[p * tq:(p + 1) * tq, c * LANES:(c + 1) * LANES]
                    a = a + jnp.maximum(sp, 0.0) * wb_ref[2 * p + half]
                accs[c] = a
        for c in range(cbs):
            adm = (kt * cbs + c) * LANES + lane < limit
            bits = lax.bitcast_convert_type(accs[c] + 0.0, I32)
            key = jnp.where(adm, bits ^ ((bits >> 31) & 0x7FFFFFFF), INT_MIN)
            sc_ref[kt * cbs + c] = key
            hi_ref[kt * cbs + c] = (key >> 16).astype(I16)
            lo_ref[kt * cbs + c] = ((key & 0xFFFF) - 2 ** 15).astype(I16)
        return carry

    n_scored = _round_up(jnp.maximum(qb + 1, cbf), max(cbs, COUNT_UNROLL))
    lax.fori_loop(0, n_scored // cbs, score_tile, 0)

    n_groups = _round_up(qb + 1, COUNT_UNROLL) // COUNT_UNROLL
    one16 = jnp.ones((tq, LANES), I16)
    zero16 = jnp.zeros((tq, LANES), I16)
    low_none = jnp.full((tq, LANES), I16_MIN, I16)

    def count16(ref, cand):
        cand_b = jnp.broadcast_to(cand, (tq, LANES)).astype(I16)

        def count(gi, acc):
            for u in range(COUNT_UNROLL):
                acc = acc + jnp.where(ref[gi * COUNT_UNROLL + u] >= cand_b, one16, zero16)
            return acc

        acc = lax.fori_loop(0, n_groups, count, zero16)
        return jnp.sum(acc.astype(F32), axis=1, keepdims=True)

    def upper_bit(i, state):
        tu, cnt_t = state
        cand_u = tu | lax.shift_left(jnp.int32(1), 15 - i)
        cnt = count16(hi_ref, cand_u - 2 ** 15)
        ok = cnt >= n_sel
        return jnp.where(ok, cand_u, tu), jnp.where(ok, cnt, cnt_t)

    tu, cnt_t = lax.fori_loop(0, 16, upper_bit,
                              (jnp.zeros((tq, 1), I32), jnp.full((tq, 1), float(2 ** 30), F32)))

    hi_b = jnp.broadcast_to(tu - 2 ** 15, (tq, LANES)).astype(I16)

    def keep_equal(gi, acc):
        for u in range(COUNT_UNROLL):
            tt = gi * COUNT_UNROLL + u
            hi = hi_ref[tt]
            lo_ref[tt] = jnp.where(hi == hi_b, lo_ref[tt], low_none)
            acc = acc + jnp.where(hi > hi_b, one16, zero16)
        return acc

    above = jnp.sum(lax.fori_loop(0, n_groups, keep_equal, zero16).astype(F32), axis=1, keepdims=True)

    def lower_bit(state):
        i, tl, cnt_t = state
        cand_l = tl | lax.shift_left(jnp.int32(1), 15 - i)
        cnt = above + count16(lo_ref, cand_l - 2 ** 15)
        ok = cnt >= n_sel
        return i + 1, jnp.where(ok, cand_l, tl), jnp.where(ok, cnt, cnt_t)

    def undecided(state):
        i, _, cnt_t = state
        return (i < 16) & (jnp.max(jnp.abs(cnt_t - n_sel)) > 0.0)

    _, tl, cnt_t = lax.while_loop(undecided, lower_bit, (jnp.int32(0), jnp.zeros((tq, 1), I32), cnt_t))
    t = ((tu - 2 ** 15) << 16) | tl
    thr_ref[...] = jnp.broadcast_to(jnp.maximum(t, INT_MIN + 1), (tq, LANES))

    tied = (cnt_t > n_sel) & (cnt_t < float(2 ** 29))

    @pl.when(jnp.max(jnp.where(tied, 1.0, 0.0)) > 0.0)
    def _():
        t_b = jnp.broadcast_to(t, (tq, LANES))
        tied_b = jnp.broadcast_to(jnp.where(tied, 1, 0), (tq, LANES)) > 0

        def count_where(pred):
            def body(tt, acc):
                return acc + jnp.where(pred(sc_ref[tt], tt * LANES + lane), 1.0, 0.0)
            acc = lax.fori_loop(0, qb + 1, body, jnp.zeros((tq, LANES), F32))
            return jnp.sum(acc, axis=1, keepdims=True)

        keep = n_sel - count_where(lambda key, kpos: key > t_b)
        n_bits = max(1, (sc_ref.shape[0] * LANES - 1).bit_length())

        def position_bit(i, cut):
            cand = cut | lax.shift_left(jnp.int32(1), n_bits - 1 - i)
            cand_b = jnp.broadcast_to(cand, (tq, LANES))
            cnt = count_where(lambda key, kpos: (key == t_b) & (kpos < cand_b))
            return jnp.where(cnt <= keep, cand, cut)

        cut = lax.fori_loop(0, n_bits, position_bit, jnp.zeros((tq, 1), I32))
        cut_b = jnp.broadcast_to(cut, (tq, LANES))

        def demote(tt, carry):
            key = sc_ref[tt]
            late = tied_b & (key == t_b) & (tt * LANES + lane >= cut_b)
            sc_ref[tt] = jnp.where(late, key - 1, key)
            return carry

        lax.fori_loop(0, qb + 1, demote, 0)

    q = q_ref[0]
    thr = thr_ref[...]
    vw = 2 * HEAD_DIM

    def values(span, kh):
        vt = v_ref[0, span, kh * HEAD_DIM:(kh + 1) * HEAD_DIM]
        one_col = lax.broadcasted_iota(I32, vt.shape, 1) == 0
        return jnp.concatenate([vt, jnp.where(one_col, 1.0, 0.0).astype(BF16)], axis=1)

    def flash_update(kh, logits, vt):
        m_prev = m_ref[kh]
        m_new = jnp.maximum(m_prev, jnp.max(logits, axis=1, keepdims=True))
        p = jnp.exp2(logits - m_new)
        acc_ref[kh] = jnp.exp2(m_prev - m_new) * acc_ref[kh] + _dot(p.astype(BF16), vt)
        m_ref[kh] = m_new

    for kh in range(N_KV_HEADS):
        for g in range(GROUP):
            c0 = (kh * GROUP + g) * HEAD_DIM
            qg_ref[kh, g * tq:(g + 1) * tq, :] = q[:, c0:c0 + HEAD_DIM]
        m_ref[kh] = jnp.full((rows, 1), NEG_BIG, F32)
        acc_ref[kh] = jnp.zeros((rows, vw), F32)

    def drop(tt, extra=0.0):
        return jnp.concatenate([jnp.where(sc_ref[tt] >= thr, 0.0, NEG_BIG) + extra] * GROUP, axis=0)

    def attend(start, width, dropped, bias):
        for kh in range(N_KV_HEADS):
            kt_h = k_ref[0, pl.ds(start, width), kh * HEAD_DIM:(kh + 1) * HEAD_DIM]
            logits = _dot_nt(qg_ref[kh], kt_h) + dropped
            if bias is not None:
                logits = logits + bias[kh]
            flash_update(kh, logits, values(pl.ds(start, width), kh))

    n_free = jnp.maximum(qb - 1, 0)

    n_far = jnp.maximum((n_free + cbf - 1) // cbf, 1)
    wf = cbf * LANES

    def far_span(j):
        t_hi = n_free - j * cbf
        t_lo = jnp.maximum(t_hi - cbf, 0)
        return t_lo, t_hi, pl.ds(pl.multiple_of(t_lo * LANES, LANES), wf)

    def far_logits(j):
        t_lo, t_hi, span = far_span(j)
        parts = [drop(t_lo + c, jnp.where(t_lo + c < t_hi, 0.0, NEG_BIG)) for c in range(cbf)]
        dropped = jnp.concatenate(parts, axis=1)
        for kh in range(N_KV_HEADS):
            logits = _dot_nt(qg_ref[kh], k_ref[0, span, kh * HEAD_DIM:(kh + 1) * HEAD_DIM]) + dropped
            m_prev = m_ref[kh]
            m_new = jnp.maximum(m_prev, jnp.max(logits, axis=1, keepdims=True))
            lg_ref[j % 2, kh] = logits
            corr_ref[j % 2, kh, 0] = jnp.exp2(m_prev - m_new)
            corr_ref[j % 2, kh, 1] = m_new
            m_ref[kh] = m_new

    def far_values(j):
        _, _, span = far_span(j)
        for kh in range(N_KV_HEADS):
            p = jnp.exp2(lg_ref[j % 2, kh] - corr_ref[j % 2, kh, 1]).astype(BF16)
            acc_ref[kh] = corr_ref[j % 2, kh, 0] * acc_ref[kh] + _dot(p, values(span, kh))

    t0 = jnp.maximum(qb - 1, 0)
    variant = jnp.where(qb == 0, 1, 0)
    attend(pl.multiple_of(t0 * LANES, LANES), 2 * LANES, jnp.concatenate([drop(t0), drop(t0 + 1)], axis=1),
           [bw_ref[variant, kh] for kh in range(N_KV_HEADS)])
    far_logits(0)

    def far_step(j, carry):
        far_values(j)
        far_logits(j + 1)
        return carry

    lax.fori_loop(0, n_far - 1, far_step, 0)
    far_values(n_far - 1)

    for kh in range(N_KV_HEADS):
        acc = acc_ref[kh]
        o = acc[:, :HEAD_DIM] / acc[:, HEAD_DIM:HEAD_DIM + 1]
        for g in range(GROUP):
            c0 = (kh * GROUP + g) * HEAD_DIM
            o_ref[0, :, c0:c0 + HEAD_DIM] = o[g * tq:(g + 1) * tq].astype(o_ref.dtype)


def _dsa(q, qi, wi, kb, vb, ki2, bw, *, tq, cbs, cbf, qb0, n_sel):
    bsz, t, aw = q.shape
    length = kb.shape[1]
    assert t % tq == 0 and length % (LANES * max(cbs, COUNT_UNROLL)) == 0 and length >= cbf * LANES
    n_tiles = length // LANES
    rows = GROUP * tq
    qspec = lambda w: pl.BlockSpec((1, tq, w), lambda b, i: (b, i, 0))
    kspec = lambda w: pl.BlockSpec((1, length, w), lambda b, i: (b, 0, 0), pipeline_mode=pl.Buffered(1))
    return pl.pallas_call(
        functools.partial(_dsa_kernel, tq=tq, cbs=cbs, cbf=cbf, qb0=qb0, n_sel=n_sel),
        grid=(bsz, t // tq),
        in_specs=[
            qspec(aw), qspec(qi.shape[2]), qspec(wi.shape[2]),
            kspec(kb.shape[2]), kspec(vb.shape[2]), kspec(ki2.shape[2]),
            pl.BlockSpec(bw.shape, lambda b, i: (0, 0, 0, 0)),
        ],
        out_specs=qspec(aw),
        out_shape=jax.ShapeDtypeStruct((bsz, t, aw), BF16),
        scratch_shapes=[
            pltpu.VMEM((n_tiles, tq, LANES), I32),
            pltpu.VMEM((n_tiles, tq, LANES), I16),
            pltpu.VMEM((n_tiles, tq, LANES), I16),
            pltpu.VMEM((IDX_HEADS, tq, LANES), F32),
            pltpu.VMEM((2, IDX_HEADS // 2 * tq, LANES), BF16),
            pltpu.VMEM((tq, LANES), I32),
            pltpu.VMEM((N_KV_HEADS, rows, HEAD_DIM), BF16),
            pltpu.VMEM((N_KV_HEADS, rows, 2 * HEAD_DIM), F32),
            pltpu.VMEM((N_KV_HEADS, rows, 1), F32),
            pltpu.VMEM((2, N_KV_HEADS, rows, cbf * LANES), F32),
            pltpu.VMEM((2, N_KV_HEADS, 2, rows, 1), F32),
        ],
        compiler_params=_params(("parallel", "arbitrary")),
        name="dsa",
    )(q, qi, wi, kb, vb, ki2, bw)


def _rel_bucket(rel):
    half = N_BUCKETS // 2
    max_exact = half // 2
    ret = jnp.where(rel > 0, half, 0)
    n = jnp.abs(rel)
    nf = jnp.maximum(n, 1).astype(F32)
    large = max_exact + (jnp.log(nf / max_exact) / math.log(MAX_DISTANCE / max_exact)
                         * (half - max_exact)).astype(I32)
    large = jnp.minimum(large, half - 1)
    return ret + jnp.where(n < max_exact, n, large)


def _bias_window(rel_bias, tq):
    i = jnp.arange(tq, dtype=I32)[:, None]
    c = jnp.arange(2 * LANES, dtype=I32)[None, :]
    rel = c - LANES - i
    far = rel_bias[N_BUCKETS // 2 - 1]
    onehot = (_rel_bucket(rel)[:, :, None] == jnp.arange(N_BUCKETS, dtype=I32)).astype(F32)
    looked_up = jnp.einsum("rcb,bh->rch", onehot, rel_bias.astype(F32), precision=lax.Precision.HIGHEST)
    bias = (looked_up - far) * LOG2E
    bias = jnp.where((rel <= -MAX_DISTANCE)[:, :, None], 0.0, bias)
    bias = bias.transpose(2, 0, 1).reshape(N_KV_HEADS, GROUP * tq, 2 * LANES)
    first = jnp.concatenate([bias[..., LANES:], jnp.zeros_like(bias[..., LANES:])], axis=-1)
    return jnp.stack([bias, first]).astype(F32)


def _split3(x):
    hi = x.astype(BF16)
    r = x - hi.astype(F32)
    mid = r.astype(BF16)
    lo = (r - mid.astype(F32)).astype(BF16)
    return hi, mid, lo


def _gla_kernel(gq_ref, gk_ref, gv_ref, la_ref, gr_ref, g_ref, s0_ref, o_ref, sf_ref, st_ref, a_ref, b_ref,
                *, chunk, n_chunks):
    step = pl.program_id(1)

    @pl.when(step == 0)
    def _():
        for h in range(GLA_HEADS):
            st_ref[h] = s0_ref[0, h].astype(F32).T

    ri = lax.broadcasted_iota(I32, (chunk, chunk), 0)
    ci = lax.broadcasted_iota(I32, (chunk, chunk), 1)
    tri = jnp.where(ci <= ri, 1.0, 0.0).astype(BF16)
    rk = lax.broadcasted_iota(I32, (chunk, GLA_DK), 0)
    levels = []
    size = chunk // 2
    while size >= GLA_SUB:
        levels.append(size)
        size //= 2

    def ref_rows(b, idx):
        out = []
        start = 0
        while start < chunk:
            end = start
            while end < chunk and idx[end] == idx[start]:
                end += 1
            out.append(jnp.broadcast_to(b[idx[start]:idx[start] + 1, :], (end - start, b.shape[1])))
            start = end
        return jnp.concatenate(out, axis=0)

    def row_of(rows8, j):
        pick = lax.broadcasted_iota(I32, rows8.shape, 0) == j % SUBLANES
        return jnp.sum(jnp.where(pick, rows8, 0.0), axis=0, keepdims=True)

    def chunk_heads(rs, exact_diagonal):
        for h in range(GLA_HEADS):
            ks = slice(h * GLA_DK, (h + 1) * GLA_DK)
            vs = slice(h * GLA_DV, (h + 1) * GLA_DV)
            qc = gq_ref[0, rs, ks]
            kc = gk_ref[0, rs, ks]
            vc = gv_ref[0, rs, vs]
            hi, mid, lo = _split3(la_ref[0, rs, ks])
            b = _dot(tri, hi) + _dot(tri, mid) + _dot(tri, lo)
            st_prev = st_ref[h]

            keep = (ri // GLA_SUB == ci // GLA_SUB) & (ci <= ri)
            if exact_diagonal:
                a_ref[...] = jnp.zeros_like(a_ref)
                b_ref[...] = b

                def column(j, carry):
                    grp = pl.multiple_of(j // SUBLANES * SUBLANES, SUBLANES)
                    bj = row_of(b_ref[pl.ds(grp, SUBLANES), :], j)
                    kj = row_of(gk_ref[0, pl.ds(rs.start + grp, SUBLANES), ks], j)
                    pair = qc * kj * jnp.exp(jnp.minimum(b - bj, 0.0))
                    col = jnp.sum(pair, axis=1, keepdims=True)
                    a_ref[...] = jnp.where(keep & (ci == j), col, a_ref[...])
                    return carry

                lax.fori_loop(0, chunk, column, 0)
                a = a_ref[...]
            else:
                bref = ref_rows(b, [(r // GLA_SUB) * GLA_SUB for r in range(chunk)])
                qd = (qc * jnp.exp(b - bref)).astype(BF16)
                kd = (kc * jnp.exp(bref - b)).astype(BF16)
                a = jnp.where(keep, _dot_nt(qd, kd), 0.0)
            for size in levels:
                bref = ref_rows(b, [(r // (2 * size)) * 2 * size + size - 1 for r in range(chunk)])
                upper = (rk // size) % 2 == 1
                ql = (qc * jnp.exp(jnp.where(upper, b - bref, 0.0))).astype(BF16)
                kl = (kc * jnp.exp(jnp.where(upper, 0.0, bref - b))).astype(BF16)
                split = (ri // (2 * size) == ci // (2 * size)) & ((ri // size) % 2 == 1) & ((ci // size) % 2 == 0)
                a = jnp.where(split, _dot_nt(ql, kl), a)

            o = _dot_nt((qc * jnp.exp(b)).astype(BF16), st_prev.astype(BF16)) + _dot(a.astype(BF16), vc)
            b_last = b[chunk - 1:chunk, :]
            kdec = (kc * jnp.exp(b_last - b)).astype(BF16)
            st_ref[h] = jnp.exp(b_last) * st_prev + lax.dot_general(
                vc, kdec, (((0,), (0,)), ((), ())), preferred_element_type=F32)

            mu = jnp.mean(o, axis=-1, keepdims=True)
            d = o - mu
            var = jnp.mean(d * d, axis=-1, keepdims=True)
            gr = gr_ref[0, rs, vs]
            y = d * lax.rsqrt(var + LN_EPS) * g_ref[h] * (gr * jax.nn.sigmoid(gr))
            o_ref[0, rs, vs] = y.astype(o_ref.dtype)

    n_blocks = _round_up(chunk * n_chunks // GLA_SUB, 2 * SUBLANES)
    bi = lax.broadcasted_iota(I32, (n_blocks, chunk * n_chunks), 0)
    rj = lax.broadcasted_iota(I32, (n_blocks, chunk * n_chunks), 1)
    in_block = jnp.where((rj // GLA_SUB == bi) & (rj % GLA_SUB != 0), 1.0, 0.0).astype(BF16)
    steep = jnp.max(_dot(in_block, (-la_ref[0]).astype(BF16))) > GLA_SAFE_EXP

    def chunks(exact_diagonal):
        def chunk_step(c, carry):
            chunk_heads(pl.ds(pl.multiple_of(c * chunk, chunk), chunk), exact_diagonal)
            return carry

        lax.fori_loop(0, n_chunks, chunk_step, 0)

    @pl.when(jnp.logical_not(steep))
    def _():
        chunks(False)

    @pl.when(steep)
    def _():
        chunks(True)

    @pl.when(step == pl.num_programs(1) - 1)
    def _():
        for h in range(GLA_HEADS):
            sf_ref[0, h] = st_ref[h].T.astype(sf_ref.dtype)


def _gla(gq, gk, gv, la, gr, g, s0, *, chunk, n_chunks):
    bsz, t, kw = gq.shape
    vw = gv.shape[2]
    tb = chunk * n_chunks
    assert t % tb == 0
    tok = lambda w: pl.BlockSpec((1, tb, w), lambda b, i: (b, i, 0))
    sspec = pl.BlockSpec((1,) + s0.shape[1:], lambda b, i: (b, 0, 0, 0))
    return pl.pallas_call(
        functools.partial(_gla_kernel, chunk=chunk, n_chunks=n_chunks),
        grid=(bsz, t // tb),
        in_specs=[tok(kw), tok(kw), tok(vw), tok(kw), tok(vw),
                  pl.BlockSpec(g.shape, lambda b, i: (0, 0, 0)), sspec],
        out_specs=[tok(vw), sspec],
        out_shape=[jax.ShapeDtypeStruct((bsz, t, vw), BF16), jax.ShapeDtypeStruct(s0.shape, s0.dtype)],
        scratch_shapes=[pltpu.VMEM((s0.shape[1], s0.shape[3], s0.shape[2]), F32),
                        pltpu.VMEM((chunk, chunk), F32), pltpu.VMEM((chunk, GLA_DK), F32)],
        compiler_params=_params(("parallel", "arbitrary")),
        name="gla",
    )(gq, gk, gv, la, gr, g, s0)


def _out_ln_kernel(x_ref, a_ref, gl_ref, woa_ref, wog_ref, g_ref, b_ref, o_ref, *, alpha):
    mix = _dot(a_ref[...], woa_ref[...]) + _dot(gl_ref[...], wog_ref[...])
    o_ref[...] = _post_norm(x_ref[...], mix, g_ref[...], b_ref[...], alpha)


def _out_ln(x, attn, gla, woa, wog, g, b, alpha, tm=512):
    n, d = x.shape
    tm = min(tm, n)
    assert n % tm == 0
    row = lambda w: pl.BlockSpec((tm, w), lambda i: (i, 0))
    full = lambda w: pl.BlockSpec(w.shape, lambda i: (0, 0))
    return pl.pallas_call(
        functools.partial(_out_ln_kernel, alpha=alpha),
        grid=(n // tm,),
        in_specs=[row(d), row(attn.shape[1]), row(gla.shape[1]), full(woa), full(wog), full(g), full(b)],
        out_specs=row(d),
        out_shape=jax.ShapeDtypeStruct((n, d), F32),
        compiler_params=_params(("parallel",)),
        name="out_ln",
    )(x, attn, gla, woa, wog, g, b)


def _matmul_kernel(x_ref, w_ref, o_ref):
    o_ref[...] = _dot(x_ref[...].astype(BF16), w_ref[...]).astype(o_ref.dtype)


def _matmul(x, w, tm=256, tn=512):
    n, d = x.shape
    m = w.shape[1]
    tm, tn = min(tm, n), min(tn, m)
    assert n % tm == 0 and m % tn == 0
    return pl.pallas_call(
        _matmul_kernel,
        grid=(n // tm, m // tn),
        in_specs=[pl.BlockSpec((tm, d), lambda i, j: (i, 0)), pl.BlockSpec((d, tn), lambda i, j: (0, j))],
        out_specs=pl.BlockSpec((tm, tn), lambda i, j: (i, j)),
        out_shape=jax.ShapeDtypeStruct((n, m), F32),
        compiler_params=_params(("parallel", "parallel")),
        name="matmul",
    )(x, w)


def _mem_ln_kernel(x_ref, wq_ref, mk_ref, mv_ref, wo_ref, g_ref, b_ref, o_ref, *, alpha):
    x = x_ref[0]
    q = _dot(x.astype(BF16), wq_ref[...])
    scale = MEM_HEAD_DIM ** -0.5
    heads = []
    for h in range(MEM_HEADS):
        hs = slice(h * MEM_HEAD_DIM, (h + 1) * MEM_HEAD_DIM)
        logits = _dot_nt(q[:, hs].astype(BF16), mk_ref[0, :, hs]) * scale
        m = jnp.max(logits, axis=-1, keepdims=True)
        p = jnp.exp(logits - m)
        p = p / jnp.sum(p, axis=-1, keepdims=True)
        heads.append(_dot(p.astype(BF16), mv_ref[0, :, hs]))
    o = jnp.concatenate(heads, axis=1).astype(BF16)
    o_ref[0] = _post_norm(x, _dot(o, wo_ref[...]), g_ref[...], b_ref[...], alpha)


def _mem_ln(x, wq, mk, mv, wo, g, b, alpha, tm=512):
    bsz, t, d = x.shape
    tm = min(tm, t)
    assert t % tm == 0
    full = lambda w: pl.BlockSpec(w.shape, lambda bb, i: (0, 0))
    mem = pl.BlockSpec((1,) + mk.shape[1:], lambda bb, i: (bb, 0, 0))
    tok = pl.BlockSpec((1, tm, d), lambda bb, i: (bb, i, 0))
    return pl.pallas_call(
        functools.partial(_mem_ln_kernel, alpha=alpha),
        grid=(bsz, t // tm),
        in_specs=[tok, full(wq), mem, mem, full(wo), full(g), full(b)],
        out_specs=tok,
        out_shape=jax.ShapeDtypeStruct((bsz, t, d), F32),
        compiler_params=_params(("parallel", "parallel")),
        name="mem_ln",
    )(x, wq, mk, mv, wo, g, b)


def _split_w_in(w_in, w_a2, b_a):
    attn_w = N_HEADS * HEAD_DIM
    kv_w = N_KV_HEADS * HEAD_DIM
    gla_kw = GLA_HEADS * GLA_DK
    gla_vw = GLA_HEADS * GLA_DV
    widths = (attn_w, kv_w, kv_w, IDX_HEADS * IDX_DIM, IDX_DIM, IDX_HEADS, gla_kw, gla_kw, gla_vw, GLA_RANK, gla_vw)
    assert sum(widths) == w_in.shape[1]
    offs = [0]
    for w in widths:
        offs.append(offs[-1] + w)
    col = lambda a, b: w_in[:, offs[a]:offs[b]].astype(BF16)
    wq, wkv, wqi = col(0, 1), col(1, 3), col(3, 4)
    wkw = jnp.pad(col(4, 6), ((0, 0), (0, LANES - IDX_DIM - IDX_HEADS)))
    wgq, wgk, wgv, wgr = col(6, 7), col(7, 8), col(8, 9), col(10, 11)
    wlr = jnp.pad(col(9, 10), ((0, 0), (0, LANES - GLA_RANK)))
    wa2 = jnp.pad(w_a2.astype(BF16), ((0, LANES - GLA_RANK), (0, 0)))
    return (wq, wqi, wkv, wkw), (wgq, wgk, wgv, wgr, wlr, wa2, b_a.reshape(1, -1).astype(F32))


def kernel(x_prompt, x_sample, cache_k, cache_v, cache_idx_k, state_gla, cache_mem_k, cache_mem_v, mem_prompt,
           rel_bias, ln_g, ln_b, ffn1_wg, ffn1_wu, ffn1_wd, w_in, w_a2, b_a, gla_norm_g, w_o, w_mq, w_mk, w_mv,
           w_mo, ffn2_wg, ffn2_wu, ffn2_wd):
    depth = w_in.shape[0]
    bp, tp, d = x_prompt.shape
    bs, ts, _ = x_sample.shape
    past = cache_k.shape[2]
    alpha = (2.0 * depth) ** 0.25
    attn_w = N_HEADS * HEAD_DIM
    kv_w = N_KV_HEADS * HEAD_DIM
    mem_w = MEM_HEADS * MEM_HEAD_DIM
    assert tp % LANES == 0 and ts == CHUNK and past % LANES == 0

    xp = x_prompt.reshape(bp * tp, d)
    xs = x_sample.reshape(bs * ts, d)
    pk, pv, pki, pS, pmk, pmv = [], [], [], [], [], []
    sk, sv, ski, sS = [], [], [], []
    bw_p = _bias_window(rel_bias, LANES)
    bw_s = bw_p.reshape(2, N_KV_HEADS, GROUP, LANES, 2 * LANES)[:, :, :, :ts].reshape(2, N_KV_HEADS, GROUP * ts, 2 * LANES)
    for l in range(depth):
        g = lambda i: ln_g[l, i].reshape(1, d)
        b = lambda i: ln_b[l, i].reshape(1, d)
        bf = lambda w: w[l].astype(BF16)
        f1 = (bf(ffn1_wg), bf(ffn1_wu), bf(ffn1_wd))
        f2 = (bf(ffn2_wg), bf(ffn2_wu), bf(ffn2_wd))
        w_attn, w_gla = _split_w_in(w_in[l], w_a2[l], b_a[l])
        woa, wog = w_o[l, :attn_w].astype(BF16), w_o[l, attn_w:].astype(BF16)
        gn = gla_norm_g[l].reshape(GLA_HEADS, 1, GLA_DV).astype(F32)

        xp = _ffn_ln(xp, *f1, g(0), b(0), alpha)
        xs = _ffn_ln(xs, *f1, g(0), b(0), alpha)

        q, qi, k, v, kb, vb, ki, ki2, wi = _attn_proj(xp, *w_attn)
        r3 = lambda a, n=bp, t=tp: a.reshape(n, t, a.shape[-1])
        attn = _dsa(r3(q), r3(qi), r3(wi), r3(kb), r3(vb), r3(ki2), bw_p,
                    tq=LANES, cbs=DSA_SCORE_TILES, cbf=DSA_FAR_TILES, qb0=0, n_sel=min(TOPK_MAX, tp // 4))
        gq, gk, gv, la, gr = _gla_proj(xp, *w_gla)
        s0 = jnp.zeros((bp, GLA_HEADS, GLA_DK, GLA_DV), state_gla.dtype)
        go, s_p = _gla(r3(gq), r3(gk), r3(gv), r3(la), r3(gr), gn, s0, chunk=CHUNK, n_chunks=4)
        xp = _out_ln(xp, attn.reshape(bp * tp, attn_w), go.reshape(bp * tp, -1), woa, wog, g(1), b(1), alpha)
        pk.append(k.reshape(bp, tp, N_KV_HEADS, HEAD_DIM))
        pv.append(v.reshape(bp, tp, N_KV_HEADS, HEAD_DIM))
        pki.append(ki.reshape(bp, tp, IDX_DIM))
        pS.append(s_p)

        q, qi, k, v, kb, vb, ki, ki2, wi = _attn_proj(xs, *w_attn)
        r3s = lambda a: a.reshape(bs, ts, a.shape[-1])
        pad_t = (-(past + ts)) % (LANES * max(DSA_SCORE_TILES, COUNT_UNROLL))
        cat = lambda c, n: jnp.pad(jnp.concatenate([c.astype(BF16), r3s(n)], axis=1), ((0, 0), (0, pad_t), (0, 0)))
        k_all = cat(cache_k[l].reshape(bs, past, kv_w), kb)
        v_all = cat(cache_v[l].reshape(bs, past, kv_w), vb)
        ki2_all = cat(jnp.concatenate([cache_idx_k[l]] * 2, axis=-1), ki2)
        attn = _dsa(r3s(q), r3s(qi), r3s(wi), k_all, v_all, ki2_all, bw_s,
                    tq=ts, cbs=DSA_SCORE_TILES, cbf=DSA_FAR_TILES, qb0=past // LANES,
                    n_sel=min(TOPK_MAX, (past + ts) // 4))
        gq, gk, gv, la, gr = _gla_proj(xs, *w_gla)
        go, s_s = _gla(r3s(gq), r3s(gk), r3s(gv), r3s(la), r3s(gr), gn, state_gla[l], chunk=ts, n_chunks=1)
        xs = _out_ln(xs, attn.reshape(bs * ts, attn_w), go.reshape(bs * ts, -1), woa, wog, g(1), b(1), alpha)
        sk.append(k.reshape(bs, ts, N_KV_HEADS, HEAD_DIM))
        sv.append(v.reshape(bs, ts, N_KV_HEADS, HEAD_DIM))
        ski.append(ki.reshape(bs, ts, IDX_DIM))
        sS.append(s_s)

        n_mem = mem_prompt.shape[1]
        mkv = _matmul(mem_prompt.reshape(bp * n_mem, d), jnp.concatenate([bf(w_mk), bf(w_mv)], axis=1))
        mk_p = mkv[:, :mem_w].reshape(bp, n_mem, mem_w)
        mv_p = mkv[:, mem_w:].reshape(bp, n_mem, mem_w)
        wmq, wmo = bf(w_mq), bf(w_mo)
        xp = _mem_ln(xp.reshape(bp, tp, d), wmq, mk_p.astype(BF16), mv_p.astype(BF16), wmo, g(2), b(2),
                     alpha).reshape(bp * tp, d)
        xs = _mem_ln(xs.reshape(bs, ts, d), wmq, cache_mem_k[l].reshape(bs, n_mem, mem_w).astype(BF16),
                     cache_mem_v[l].reshape(bs, n_mem, mem_w).astype(BF16), wmo, g(2), b(2),
                     alpha).reshape(bs * ts, d)
        pmk.append(mk_p.reshape(bp, n_mem, MEM_HEADS, MEM_HEAD_DIM))
        pmv.append(mv_p.reshape(bp, n_mem, MEM_HEADS, MEM_HEAD_DIM))

        xp = _ffn_ln(xp, *f2, g(3), b(3), alpha)
        xs = _ffn_ln(xs, *f2, g(3), b(3), alpha)

    return (xp.reshape(bp, tp, d), xs.reshape(bs, ts, d),
            jnp.stack(pk), jnp.stack(pv), jnp.stack(pki), jnp.stack(pS), jnp.stack(pmk), jnp.stack(pmv),
            jnp.stack(sk), jnp.stack(sv), jnp.stack(ski), jnp.stack(sS))
```

```python
import functools
import math

import jax
import jax.numpy as jnp
from jax import lax
from jax.experimental import pallas as pl
from jax.experimental.pallas import tpu as pltpu

F32 = jnp.float32
BF16 = jnp.bfloat16
I32 = jnp.int32

LANES = 128
SUBLANES = 8
VMEM_BYTES_V7X = 64 * 1024 * 1024
VMEM_LIMIT = VMEM_BYTES_V7X - 4 * 1024 * 1024

CHUNK = 64
N_HEADS = 8
N_KV_HEADS = 2
HEAD_DIM = 128
GROUP = N_HEADS // N_KV_HEADS
IDX_HEADS = 16
IDX_DIM = 64
TOPK_MAX = 256
GLA_HEADS = 4
GLA_DK = 128
GLA_DV = 256
GLA_RANK = 16
GLA_NORMALIZER = 16.0
MEM_HEADS = 4
MEM_HEAD_DIM = 128
N_BUCKETS = 32
MAX_DISTANCE = 128
LN_EPS = 1e-5
INT_MIN = -2 ** 31
NEG_BIG = -1e30
GLA_SUB = 16
GLA_SAFE_EXP = 60.0
COUNT_UNROLL = 8
DSA_SCORE_TILES = 8
DSA_FAR_TILES = 8
LOG2E = math.log2(math.e)


def _round_up(x, m):
    return (x + m - 1) // m * m


def _dot(a, b):
    return jnp.dot(a, b, preferred_element_type=F32)


def _dot_nt(a, b):
    return lax.dot_general(a, b, (((1,), (1,)), ((), ())), preferred_element_type=F32)


def _params(semantics):
    return pltpu.CompilerParams(dimension_semantics=semantics, vmem_limit_bytes=VMEM_LIMIT)


def _post_norm(x, sub, g, b, alpha):
    h = alpha * x + sub
    mu = jnp.mean(h, axis=-1, keepdims=True)
    d = h - mu
    var = jnp.mean(d * d, axis=-1, keepdims=True)
    return d * lax.rsqrt(var + LN_EPS) * g + b


def _ffn_ln_kernel(x_ref, wg_ref, wu_ref, wd_ref, g_ref, b_ref, o_ref, xb_ref, acc_ref, *, alpha):
    j = pl.program_id(1)

    @pl.when(j == 0)
    def _():
        xb_ref[...] = x_ref[...].astype(BF16)
        acc_ref[...] = jnp.zeros_like(acc_ref)

    xb = xb_ref[...]
    hg = _dot(xb, wg_ref[...])
    hu = _dot(xb, wu_ref[...])
    h = (hg * jax.nn.sigmoid(hg)) * hu
    acc_ref[...] += _dot(h.astype(BF16), wd_ref[...])

    @pl.when(j == pl.num_programs(1) - 1)
    def _():
        o_ref[...] = _post_norm(x_ref[...], 0.5 * acc_ref[...], g_ref[...], b_ref[...], alpha)


def _ffn_ln(x, wg, wu, wd, g, b, alpha, tm=512, tf=512):
    n, d = x.shape
    f = wg.shape[1]
    tm = min(tm, n)
    assert n % tm == 0 and f % tf == 0
    return pl.pallas_call(
        functools.partial(_ffn_ln_kernel, alpha=alpha),
        grid=(n // tm, f // tf),
        in_specs=[
            pl.BlockSpec((tm, d), lambda i, j: (i, 0)),
            pl.BlockSpec((d, tf), lambda i, j: (0, j)),
            pl.BlockSpec((d, tf), lambda i, j: (0, j)),
            pl.BlockSpec((tf, d), lambda i, j: (j, 0)),
            pl.BlockSpec((1, d), lambda i, j: (0, 0)),
            pl.BlockSpec((1, d), lambda i, j: (0, 0)),
        ],
        out_specs=pl.BlockSpec((tm, d), lambda i, j: (i, 0)),
        out_shape=jax.ShapeDtypeStruct((n, d), F32),
        scratch_shapes=[pltpu.VMEM((tm, d), BF16), pltpu.VMEM((tm, d), F32)],
        compiler_params=_params(("parallel", "arbitrary")),
        name="ffn_ln",
    )(x, wg, wu, wd, g, b)


def _attn_proj_kernel(x_ref, wq_ref, wqi_ref, wkv_ref, wkw_ref,
                      q_ref, qi_ref, k_ref, v_ref, kb_ref, vb_ref, ki_ref, ki2_ref, wi_ref):
    xb = x_ref[...].astype(BF16)
    q_ref[...] = (_dot(xb, wq_ref[...]) * (HEAD_DIM ** -0.5 * LOG2E)).astype(BF16)
    qi_ref[...] = _dot(xb, wqi_ref[...]).astype(BF16)
    kv = _dot(xb, wkv_ref[...])
    nkv = kv.shape[1] // 2
    k_ref[...] = kv[:, :nkv]
    v_ref[...] = kv[:, nkv:]
    kb_ref[...] = kv[:, :nkv].astype(BF16)
    vb_ref[...] = kv[:, nkv:].astype(BF16)
    kw = _dot(xb, wkw_ref[...])
    ki_ref[...] = kw[:, :IDX_DIM]
    wi_ref[...] = kw[:, IDX_DIM:IDX_DIM + IDX_HEADS]
    lane = lax.broadcasted_iota(I32, kw.shape, 1)
    ki2_ref[...] = jnp.where(lane < IDX_DIM, kw, pltpu.roll(kw, IDX_DIM, 1)).astype(BF16)


def _attn_proj(x, wq, wqi, wkv, wkw, tm=512):
    n, d = x.shape
    tm = min(tm, n)
    assert n % tm == 0
    row = lambda w: pl.BlockSpec((tm, w), lambda i: (i, 0))
    full = lambda w: pl.BlockSpec(w.shape, lambda i: (0, 0))
    nq, nqi, nkv = wq.shape[1], wqi.shape[1], wkv.shape[1] // 2
    outs = [
        (nq, BF16), (nqi, BF16), (nkv, F32), (nkv, F32), (nkv, BF16), (nkv, BF16),
        (IDX_DIM, F32), (LANES, BF16), (IDX_HEADS, F32),
    ]
    return pl.pallas_call(
        _attn_proj_kernel,
        grid=(n // tm,),
        in_specs=[row(d), full(wq), full(wqi), full(wkv), full(wkw)],
        out_specs=[row(w) for w, _ in outs],
        out_shape=[jax.ShapeDtypeStruct((n, w), t) for w, t in outs],
        compiler_params=_params(("parallel",)),
        name="attn_proj",
    )(x, wq, wqi, wkv, wkw)


def _gla_proj_kernel(x_ref, wgq_ref, wgk_ref, wgv_ref, wgr_ref, wlr_ref, wa2_ref, ba_ref,
                     gq_ref, gk_ref, gv_ref, la_ref, gr_ref):
    xb = x_ref[...].astype(BF16)
    gq_ref[...] = _dot(xb, wgq_ref[...]) * (GLA_DK ** -0.5)
    gk_ref[...] = _dot(xb, wgk_ref[...])
    gv_ref[...] = _dot(xb, wgv_ref[...]).astype(BF16)
    gr_ref[...] = _dot(xb, wgr_ref[...])
    lr = _dot(xb, wlr_ref[...])
    z = _dot(lr.astype(BF16), wa2_ref[...]) + ba_ref[...]
    la_ref[...] = (jnp.minimum(z, 0.0) - jnp.log(1.0 + jnp.exp(-jnp.abs(z)))) * (1.0 / GLA_NORMALIZER)


def _gla_proj(x, wgq, wgk, wgv, wgr, wlr, wa2, ba, tm=256):
    n, d = x.shape
    tm = min(tm, n)
    assert n % tm == 0
    row = lambda w: pl.BlockSpec((tm, w), lambda i: (i, 0))
    full = lambda w: pl.BlockSpec(w.shape, lambda i: (0, 0))
    kw, vw = wgq.shape[1], wgv.shape[1]
    outs = [(kw, F32), (kw, F32), (vw, BF16), (kw, F32), (vw, F32)]
    return pl.pallas_call(
        _gla_proj_kernel,
        grid=(n // tm,),
        in_specs=[row(d), full(wgq), full(wgk), full(wgv), full(wgr), full(wlr), full(wa2), full(ba)],
        out_specs=[row(w) for w, _ in outs],
        out_shape=[jax.ShapeDtypeStruct((n, w), t) for w, t in outs],
        compiler_params=_params(("parallel",)),
        name="gla_proj",
    )(x, wgq, wgk, wgv, wgr, wlr, wa2, ba)


def _sortable_key(a):
    bits = lax.bitcast_convert_type(a + 0.0, I32)
    return bits ^ ((bits >> 31) & 0x7FFFFFFF)


def _dsa_kernel(q_ref, qi_ref, wi_ref, k_ref, v_ref, ki2_ref, bw_ref, o_ref,
                sc_ref, sc16_ref, wb_ref, qip_ref, thr_ref, qg_ref, acc_ref, m_ref, lg_ref, corr_ref,
                *, tq, cbs, cbf, qb0, n_sel):
    qb = pl.program_id(1) + qb0
    q0 = qb * LANES
    rows = GROUP * tq

    wi = wi_ref[0]
    for h in range(IDX_HEADS):
        wb_ref[h] = jnp.broadcast_to(wi[:, h:h + 1], (tq, LANES))
    qi = qi_ref[0]
    row = lax.broadcasted_iota(I32, (tq, LANES), 0)
    lane = lax.broadcasted_iota(I32, (tq, LANES), 1)
    for p in range(IDX_HEADS // 2):
        pair = qi[:, p * LANES:(p + 1) * LANES]
        qip_ref[0, p * tq:(p + 1) * tq, :] = jnp.where(lane < IDX_DIM, pair, jnp.zeros_like(pair))
        qip_ref[1, p * tq:(p + 1) * tq, :] = jnp.where(lane >= IDX_DIM, pair, jnp.zeros_like(pair))
    limit = q0 + (row // CHUNK + 1) * CHUNK
    tks = cbs * LANES

    def score_tile(kt, carry):
        start = pl.multiple_of(kt * tks, tks)
        ki2 = ki2_ref[0, pl.ds(start, tks), :]
        accs = [jnp.zeros((tq, LANES), F32) for _ in range(cbs)]
        for half in range(2):
            s = _dot_nt(qip_ref[half], ki2)
            for c in range(cbs):
                a = accs[c]
                for p in range(IDX_HEADS // 2):
                    sp = s[p * tq:(p + 1) * tq, c * LANES:(c + 1) * LANES]
                    a = a + jnp.maximum(sp, 0.0) * wb_ref[2 * p + half]
                accs[c] = a
        for c in range(cbs):
            adm = (kt * cbs + c) * LANES + lane < limit
            bits = lax.bitcast_convert_type(accs[c] + 0.0, I32)
            key = jnp.where(adm, bits ^ ((bits >> 31) & 0x7FFFFFFF), INT_MIN)
            sc_ref[kt * cbs + c] = key
            sc16_ref[kt * cbs + c] = (key >> 16).astype(jnp.int16)
        return carry

    n_scored = _round_up(jnp.maximum(qb + 1, cbf), max(cbs, COUNT_UNROLL))
    lax.fori_loop(0, n_scored // cbs, score_tile, 0)

    one16 = jnp.ones((tq, LANES), jnp.int16)
    zero16 = jnp.zeros((tq, LANES), jnp.int16)

    def search(keys_ref, keys16_ref, n_groups):
        def upper_bit(i, state):
            tu, cnt_t = state
            cand_u = tu | lax.shift_left(jnp.int32(1), 15 - i)
            cand_b = jnp.broadcast_to(cand_u - 2 ** 15, (tq, LANES)).astype(jnp.int16)

            def count(gi, acc):
                for u in range(COUNT_UNROLL):
                    acc = acc + jnp.where(keys16_ref[gi * COUNT_UNROLL + u] >= cand_b, one16, zero16)
                return acc

            acc = lax.fori_loop(0, n_groups, count, zero16)
            cnt = jnp.sum(acc.astype(F32), axis=1, keepdims=True)
            ok = cnt >= n_sel
            return jnp.where(ok, cand_u, tu), jnp.where(ok, cnt, cnt_t)

        tu, cnt_t = lax.fori_loop(0, 16, upper_bit,
                                  (jnp.zeros((tq, 1), I32), jnp.full((tq, 1), float(2 ** 30), F32)))

        def lower_bit(state):
            i, t, cnt_t = state
            cand = t | lax.shift_left(jnp.int32(1), 15 - i)
            cand_b = jnp.broadcast_to(cand, (tq, LANES))

            def count(gi, acc):
                for u in range(COUNT_UNROLL):
                    acc = acc + jnp.where(keys_ref[gi * COUNT_UNROLL + u] >= cand_b, 1.0, 0.0)
                return acc

            acc = lax.fori_loop(0, n_groups, count, jnp.zeros((tq, LANES), F32))
            cnt = jnp.sum(acc, axis=1, keepdims=True)
            ok = cnt >= n_sel
            return i + 1, jnp.where(ok, cand, t), jnp.where(ok, cnt, cnt_t)

        def undecided(state):
            i, _, cnt_t = state
            return (i < 16) & (jnp.max(jnp.abs(cnt_t - n_sel)) > 0.0)

        _, t, cnt_t = lax.while_loop(undecided, lower_bit, (jnp.int32(0), (tu - 2 ** 15) << 16, cnt_t))
        return t, cnt_t

    t, cnt_t = search(sc_ref, sc16_ref, _round_up(qb + 1, COUNT_UNROLL) // COUNT_UNROLL)
    thr_ref[...] = jnp.broadcast_to(jnp.maximum(t, INT_MIN + 1), (tq, LANES))

    tied = (cnt_t > n_sel) & (cnt_t < float(2 ** 29))

    @pl.when(jnp.max(jnp.where(tied, 1.0, 0.0)) > 0.0)
    def _():
        t_b = jnp.broadcast_to(t, (tq, LANES))
        tied_b = jnp.broadcast_to(jnp.where(tied, 1, 0), (tq, LANES)) > 0

        def count_where(pred):
            def body(tt, acc):
                return acc + jnp.where(pred(sc_ref[tt], tt * LANES + lane), 1.0, 0.0)
            acc = lax.fori_loop(0, qb + 1, body, jnp.zeros((tq, LANES), F32))
            return jnp.sum(acc, axis=1, keepdims=True)

        keep = n_sel - count_where(lambda key, kpos: key > t_b)
        n_bits = max(1, (sc_ref.shape[0] * LANES - 1).bit_length())

        def position_bit(i, cut):
            cand = cut | lax.shift_left(jnp.int32(1), n_bits - 1 - i)
            cand_b = jnp.broadcast_to(cand, (tq, LANES))
            cnt = count_where(lambda key, kpos: (key == t_b) & (kpos < cand_b))
            return jnp.where(cnt <= keep, cand, cut)

        cut = lax.fori_loop(0, n_bits, position_bit, jnp.zeros((tq, 1), I32))
        cut_b = jnp.broadcast_to(cut, (tq, LANES))

        def demote(tt, carry):
            key = sc_ref[tt]
            late = tied_b & (key == t_b) & (tt * LANES + lane >= cut_b)
            sc_ref[tt] = jnp.where(late, key - 1, key)
            return carry

        lax.fori_loop(0, qb + 1, demote, 0)

    q = q_ref[0]
    thr = thr_ref[...]
    vw = 2 * HEAD_DIM

    def values(span, kh):
        vt = v_ref[0, span, kh * HEAD_DIM:(kh + 1) * HEAD_DIM]
        one_col = lax.broadcasted_iota(I32, vt.shape, 1) == 0
        return jnp.concatenate([vt, jnp.where(one_col, 1.0, 0.0).astype(BF16)], axis=1)

    def flash_update(kh, logits, vt):
        m_prev = m_ref[kh]
        m_new = jnp.maximum(m_prev, jnp.max(logits, axis=1, keepdims=True))
        p = jnp.exp2(logits - m_new)
        acc_ref[kh] = jnp.exp2(m_prev - m_new) * acc_ref[kh] + _dot(p.astype(BF16), vt)
        m_ref[kh] = m_new

    for kh in range(N_KV_HEADS):
        for g in range(GROUP):
            c0 = (kh * GROUP + g) * HEAD_DIM
            qg_ref[kh, g * tq:(g + 1) * tq, :] = q[:, c0:c0 + HEAD_DIM]
        m_ref[kh] = jnp.full((rows, 1), NEG_BIG, F32)
        acc_ref[kh] = jnp.zeros((rows, vw), F32)

    def drop(tt, extra=0.0):
        return jnp.concatenate([jnp.where(sc_ref[tt] >= thr, 0.0, NEG_BIG) + extra] * GROUP, axis=0)

    def attend(start, width, dropped, bias):
        for kh in range(N_KV_HEADS):
            kt_h = k_ref[0, pl.ds(start, width), kh * HEAD_DIM:(kh + 1) * HEAD_DIM]
            logits = _dot_nt(qg_ref[kh], kt_h) + dropped
            if bias is not None:
                logits = logits + bias[kh]
            flash_update(kh, logits, values(pl.ds(start, width), kh))

    n_free = jnp.maximum(qb - 1, 0)

    n_far = jnp.maximum((n_free + cbf - 1) // cbf, 1)
    wf = cbf * LANES

    def far_span(j):
        t_hi = n_free - j * cbf
        t_lo = jnp.maximum(t_hi - cbf, 0)
        return t_lo, t_hi, pl.ds(pl.multiple_of(t_lo * LANES, LANES), wf)

    def far_logits(j):
        t_lo, t_hi, span = far_span(j)
        parts = [drop(t_lo + c, jnp.where(t_lo + c < t_hi, 0.0, NEG_BIG)) for c in range(cbf)]
        dropped = jnp.concatenate(parts, axis=1)
        for kh in range(N_KV_HEADS):
            logits = _dot_nt(qg_ref[kh], k_ref[0, span, kh * HEAD_DIM:(kh + 1) * HEAD_DIM]) + dropped
            m_prev = m_ref[kh]
            m_new = jnp.maximum(m_prev, jnp.max(logits, axis=1, keepdims=True))
            lg_ref[j % 2, kh] = logits
            corr_ref[j % 2, kh, 0] = jnp.exp2(m_prev - m_new)
            corr_ref[j % 2, kh, 1] = m_new
            m_ref[kh] = m_new

    def far_values(j):
        _, _, span = far_span(j)
        for kh in range(N_KV_HEADS):
            p = jnp.exp2(lg_ref[j % 2, kh] - corr_ref[j % 2, kh, 1]).astype(BF16)
            acc_ref[kh] = corr_ref[j % 2, kh, 0] * acc_ref[kh] + _dot(p, values(span, kh))

    t0 = jnp.maximum(qb - 1, 0)
    variant = jnp.where(qb == 0, 1, 0)
    attend(pl.multiple_of(t0 * LANES, LANES), 2 * LANES, jnp.concatenate([drop(t0), drop(t0 + 1)], axis=1),
           [bw_ref[variant, kh] for kh in range(N_KV_HEADS)])
    far_logits(0)

    def far_step(j, carry):
        far_values(j)
        far_logits(j + 1)
        return carry

    lax.fori_loop(0, n_far - 1, far_step, 0)
    far_values(n_far - 1)

    for kh in range(N_KV_HEADS):
        acc = acc_ref[kh]
        o = acc[:, :HEAD_DIM] / acc[:, HEAD_DIM:HEAD_DIM + 1]
        for g in range(GROUP):
            c0 = (kh * GROUP + g) * HEAD_DIM
            o_ref[0, :, c0:c0 + HEAD_DIM] = o[g * tq:(g + 1) * tq].astype(o_ref.dtype)


def _dsa(q, qi, wi, kb, vb, ki2, bw, *, tq, cbs, cbf, qb0, n_sel):
    bsz, t, aw = q.shape
    length = kb.shape[1]
    assert t % tq == 0 and length % (LANES * max(cbs, COUNT_UNROLL)) == 0 and length >= cbf * LANES
    n_tiles = length // LANES
    rows = GROUP * tq
    qspec = lambda w: pl.BlockSpec((1, tq, w), lambda b, i: (b, i, 0))
    kspec = lambda w: pl.BlockSpec((1, length, w), lambda b, i: (b, 0, 0), pipeline_mode=pl.Buffered(1))
    return pl.pallas_call(
        functools.partial(_dsa_kernel, tq=tq, cbs=cbs, cbf=cbf, qb0=qb0, n_sel=n_sel),
        grid=(bsz, t // tq),
        in_specs=[
            qspec(aw), qspec(qi.shape[2]), qspec(wi.shape[2]),
            kspec(kb.shape[2]), kspec(vb.shape[2]), kspec(ki2.shape[2]),
            pl.BlockSpec(bw.shape, lambda b, i: (0, 0, 0, 0)),
        ],
        out_specs=qspec(aw),
        out_shape=jax.ShapeDtypeStruct((bsz, t, aw), BF16),
        scratch_shapes=[
            pltpu.VMEM((n_tiles, tq, LANES), I32),
            pltpu.VMEM((n_tiles, tq, LANES), jnp.int16),
            pltpu.VMEM((IDX_HEADS, tq, LANES), F32),
            pltpu.VMEM((2, IDX_HEADS // 2 * tq, LANES), BF16),
            pltpu.VMEM((tq, LANES), I32),
            pltpu.VMEM((N_KV_HEADS, rows, HEAD_DIM), BF16),
            pltpu.VMEM((N_KV_HEADS, rows, 2 * HEAD_DIM), F32),
            pltpu.VMEM((N_KV_HEADS, rows, 1), F32),
            pltpu.VMEM((2, N_KV_HEADS, rows, cbf * LANES), F32),
            pltpu.VMEM((2, N_KV_HEADS, 2, rows, 1), F32),
        ],
        compiler_params=_params(("parallel", "arbitrary")),
        name="dsa",
    )(q, qi, wi, kb, vb, ki2, bw)


def _rel_bucket(rel):
    half = N_BUCKETS // 2
    max_exact = half // 2
    ret = jnp.where(rel > 0, half, 0)
    n = jnp.abs(rel)
    nf = jnp.maximum(n, 1).astype(F32)
    large = max_exact + (jnp.log(nf / max_exact) / math.log(MAX_DISTANCE / max_exact)
                         * (half - max_exact)).astype(I32)
    large = jnp.minimum(large, half - 1)
    return ret + jnp.where(n < max_exact, n, large)


def _bias_window(rel_bias, tq):
    i = jnp.arange(tq, dtype=I32)[:, None]
    c = jnp.arange(2 * LANES, dtype=I32)[None, :]
    rel = c - LANES - i
    far = rel_bias[N_BUCKETS // 2 - 1]
    onehot = (_rel_bucket(rel)[:, :, None] == jnp.arange(N_BUCKETS, dtype=I32)).astype(F32)
    looked_up = jnp.einsum("rcb,bh->rch", onehot, rel_bias.astype(F32), precision=lax.Precision.HIGHEST)
    bias = (looked_up - far) * LOG2E
    bias = jnp.where((rel <= -MAX_DISTANCE)[:, :, None], 0.0, bias)
    bias = bias.transpose(2, 0, 1).reshape(N_KV_HEADS, GROUP * tq, 2 * LANES)
    first = jnp.concatenate([bias[..., LANES:], jnp.zeros_like(bias[..., LANES:])], axis=-1)
    return jnp.stack([bias, first]).astype(F32)


def _split3(x):
    hi = x.astype(BF16)
    r = x - hi.astype(F32)
    mid = r.astype(BF16)
    lo = (r - mid.astype(F32)).astype(BF16)
    return hi, mid, lo


def _gla_kernel(gq_ref, gk_ref, gv_ref, la_ref, gr_ref, g_ref, s0_ref, o_ref, sf_ref, st_ref, a_ref, b_ref,
                *, chunk, n_chunks):
    step = pl.program_id(1)

    @pl.when(step == 0)
    def _():
        for h in range(GLA_HEADS):
            st_ref[h] = s0_ref[0, h].astype(F32).T

    ri = lax.broadcasted_iota(I32, (chunk, chunk), 0)
    ci = lax.broadcasted_iota(I32, (chunk, chunk), 1)
    tri = jnp.where(ci <= ri, 1.0, 0.0).astype(BF16)
    rk = lax.broadcasted_iota(I32, (chunk, GLA_DK), 0)
    levels = []
    size = chunk // 2
    while size >= GLA_SUB:
        levels.append(size)
        size //= 2

    def ref_rows(b, idx):
        out = []
        start = 0
        while start < chunk:
            end = start
            while end < chunk and idx[end] == idx[start]:
                end += 1
            out.append(jnp.broadcast_to(b[idx[start]:idx[start] + 1, :], (end - start, b.shape[1])))
            start = end
        return jnp.concatenate(out, axis=0)

    def row_of(rows8, j):
        pick = lax.broadcasted_iota(I32, rows8.shape, 0) == j % SUBLANES
        return jnp.sum(jnp.where(pick, rows8, 0.0), axis=0, keepdims=True)

    def chunk_heads(rs, exact_diagonal):
        for h in range(GLA_HEADS):
            ks = slice(h * GLA_DK, (h + 1) * GLA_DK)
            vs = slice(h * GLA_DV, (h + 1) * GLA_DV)
            qc = gq_ref[0, rs, ks]
            kc = gk_ref[0, rs, ks]
            vc = gv_ref[0, rs, vs]
            hi, mid, lo = _split3(la_ref[0, rs, ks])
            b = _dot(tri, hi) + _dot(tri, mid) + _dot(tri, lo)
            st_prev = st_ref[h]

            keep = (ri // GLA_SUB == ci // GLA_SUB) & (ci <= ri)
            if exact_diagonal:
                a_ref[...] = jnp.zeros_like(a_ref)
                b_ref[...] = b

                def column(j, carry):
                    grp = pl.multiple_of(j // SUBLANES * SUBLANES, SUBLANES)
                    bj = row_of(b_ref[pl.ds(grp, SUBLANES), :], j)
                    kj = row_of(gk_ref[0, pl.ds(rs.start + grp, SUBLANES), ks], j)
                    pair = qc * kj * jnp.exp(jnp.minimum(b - bj, 0.0))
                    col = jnp.sum(pair, axis=1, keepdims=True)
                    a_ref[...] = jnp.where(keep & (ci == j), col, a_ref[...])
                    return carry

                lax.fori_loop(0, chunk, column, 0)
                a = a_ref[...]
            else:
                bref = ref_rows(b, [(r // GLA_SUB) * GLA_SUB for r in range(chunk)])
                qd = (qc * jnp.exp(b - bref)).astype(BF16)
                kd = (kc * jnp.exp(bref - b)).astype(BF16)
                a = jnp.where(keep, _dot_nt(qd, kd), 0.0)
            for size in levels:
                bref = ref_rows(b, [(r // (2 * size)) * 2 * size + size - 1 for r in range(chunk)])
                upper = (rk // size) % 2 == 1
                ql = (qc * jnp.exp(jnp.where(upper, b - bref, 0.0))).astype(BF16)
                kl = (kc * jnp.exp(jnp.where(upper, 0.0, bref - b))).astype(BF16)
                split = (ri // (2 * size) == ci // (2 * size)) & ((ri // size) % 2 == 1) & ((ci // size) % 2 == 0)
                a = jnp.where(split, _dot_nt(ql, kl), a)

            o = _dot_nt((qc * jnp.exp(b)).astype(BF16), st_prev.astype(BF16)) + _dot(a.astype(BF16), vc)
            b_last = b[chunk - 1:chunk, :]
            kdec = (kc * jnp.exp(b_last - b)).astype(BF16)
            st_ref[h] = jnp.exp(b_last) * st_prev + lax.dot_general(
                vc, kdec, (((0,), (0,)), ((), ())), preferred_element_type=F32)

            mu = jnp.mean(o, axis=-1, keepdims=True)
            d = o - mu
            var = jnp.mean(d * d, axis=-1, keepdims=True)
            gr = gr_ref[0, rs, vs]
            y = d * lax.rsqrt(var + LN_EPS) * g_ref[h] * (gr * jax.nn.sigmoid(gr))
            o_ref[0, rs, vs] = y.astype(o_ref.dtype)

    n_blocks = _round_up(chunk * n_chunks // GLA_SUB, 2 * SUBLANES)
    bi = lax.broadcasted_iota(I32, (n_blocks, chunk * n_chunks), 0)
    rj = lax.broadcasted_iota(I32, (n_blocks, chunk * n_chunks), 1)
    in_block = jnp.where((rj // GLA_SUB == bi) & (rj % GLA_SUB != 0), 1.0, 0.0).astype(BF16)
    steep = jnp.max(_dot(in_block, (-la_ref[0]).astype(BF16))) > GLA_SAFE_EXP

    def chunks(exact_diagonal):
        def chunk_step(c, carry):
            chunk_heads(pl.ds(pl.multiple_of(c * chunk, chunk), chunk), exact_diagonal)
            return carry

        lax.fori_loop(0, n_chunks, chunk_step, 0)

    @pl.when(jnp.logical_not(steep))
    def _():
        chunks(False)

    @pl.when(steep)
    def _():
        chunks(True)

    @pl.when(step == pl.num_programs(1) - 1)
    def _():
        for h in range(GLA_HEADS):
            sf_ref[0, h] = st_ref[h].T.astype(sf_ref.dtype)


def _gla(gq, gk, gv, la, gr, g, s0, *, chunk, n_chunks):
    bsz, t, kw = gq.shape
    vw = gv.shape[2]
    tb = chunk * n_chunks
    assert t % tb == 0
    tok = lambda w: pl.BlockSpec((1, tb, w), lambda b, i: (b, i, 0))
    sspec = pl.BlockSpec((1,) + s0.shape[1:], lambda b, i: (b, 0, 0, 0))
    return pl.pallas_call(
        functools.partial(_gla_kernel, chunk=chunk, n_chunks=n_chunks),
        grid=(bsz, t // tb),
        in_specs=[tok(kw), tok(kw), tok(vw), tok(kw), tok(vw),
                  pl.BlockSpec(g.shape, lambda b, i: (0, 0, 0)), sspec],
        out_specs=[tok(vw), sspec],
        out_shape=[jax.ShapeDtypeStruct((bsz, t, vw), BF16), jax.ShapeDtypeStruct(s0.shape, s0.dtype)],
        scratch_shapes=[pltpu.VMEM((s0.shape[1], s0.shape[3], s0.shape[2]), F32),
                        pltpu.VMEM((chunk, chunk), F32), pltpu.VMEM((chunk, GLA_DK), F32)],
        compiler_params=_params(("parallel", "arbitrary")),
        name="gla",
    )(gq, gk, gv, la, gr, g, s0)


def _out_ln_kernel(x_ref, a_ref, gl_ref, woa_ref, wog_ref, g_ref, b_ref, o_ref, *, alpha):
    mix = _dot(a_ref[...], woa_ref[...]) + _dot(gl_ref[...], wog_ref[...])
    o_ref[...] = _post_norm(x_ref[...], mix, g_ref[...], b_ref[...], alpha)


def _out_ln(x, attn, gla, woa, wog, g, b, alpha, tm=512):
    n, d = x.shape
    tm = min(tm, n)
    assert n % tm == 0
    row = lambda w: pl.BlockSpec((tm, w), lambda i: (i, 0))
    full = lambda w: pl.BlockSpec(w.shape, lambda i: (0, 0))
    return pl.pallas_call(
        functools.partial(_out_ln_kernel, alpha=alpha),
        grid=(n // tm,),
        in_specs=[row(d), row(attn.shape[1]), row(gla.shape[1]), full(woa), full(wog), full(g), full(b)],
        out_specs=row(d),
        out_shape=jax.ShapeDtypeStruct((n, d), F32),
        compiler_params=_params(("parallel",)),
        name="out_ln",
    )(x, attn, gla, woa, wog, g, b)


def _matmul_kernel(x_ref, w_ref, o_ref):
    o_ref[...] = _dot(x_ref[...].astype(BF16), w_ref[...]).astype(o_ref.dtype)


def _matmul(x, w, tm=256, tn=512):
    n, d = x.shape
    m = w.shape[1]
    tm, tn = min(tm, n), min(tn, m)
    assert n % tm == 0 and m % tn == 0
    return pl.pallas_call(
        _matmul_kernel,
        grid=(n // tm, m // tn),
        in_specs=[pl.BlockSpec((tm, d), lambda i, j: (i, 0)), pl.BlockSpec((d, tn), lambda i, j: (0, j))],
        out_specs=pl.BlockSpec((tm, tn), lambda i, j: (i, j)),
        out_shape=jax.ShapeDtypeStruct((n, m), F32),
        compiler_params=_params(("parallel", "parallel")),
        name="matmul",
    )(x, w)


def _mem_ln_kernel(x_ref, wq_ref, mk_ref, mv_ref, wo_ref, g_ref, b_ref, o_ref, *, alpha):
    x = x_ref[0]
    q = _dot(x.astype(BF16), wq_ref[...])
    scale = MEM_HEAD_DIM ** -0.5
    heads = []
    for h in range(MEM_HEADS):
        hs = slice(h * MEM_HEAD_DIM, (h + 1) * MEM_HEAD_DIM)
        logits = _dot_nt(q[:, hs].astype(BF16), mk_ref[0, :, hs]) * scale
        m = jnp.max(logits, axis=-1, keepdims=True)
        p = jnp.exp(logits - m)
        p = p / jnp.sum(p, axis=-1, keepdims=True)
        heads.append(_dot(p.astype(BF16), mv_ref[0, :, hs]))
    o = jnp.concatenate(heads, axis=1).astype(BF16)
    o_ref[0] = _post_norm(x, _dot(o, wo_ref[...]), g_ref[...], b_ref[...], alpha)


def _mem_ln(x, wq, mk, mv, wo, g, b, alpha, tm=512):
    bsz, t, d = x.shape
    tm = min(tm, t)
    assert t % tm == 0
    full = lambda w: pl.BlockSpec(w.shape, lambda bb, i: (0, 0))
    mem = pl.BlockSpec((1,) + mk.shape[1:], lambda bb, i: (bb, 0, 0))
    tok = pl.BlockSpec((1, tm, d), lambda bb, i: (bb, i, 0))
    return pl.pallas_call(
        functools.partial(_mem_ln_kernel, alpha=alpha),
        grid=(bsz, t // tm),
        in_specs=[tok, full(wq), mem, mem, full(wo), full(g), full(b)],
        out_specs=tok,
        out_shape=jax.ShapeDtypeStruct((bsz, t, d), F32),
        compiler_params=_params(("parallel", "parallel")),
        name="mem_ln",
    )(x, wq, mk, mv, wo, g, b)


def _split_w_in(w_in, w_a2, b_a):
    attn_w = N_HEADS * HEAD_DIM
    kv_w = N_KV_HEADS * HEAD_DIM
    gla_kw = GLA_HEADS * GLA_DK
    gla_vw = GLA_HEADS * GLA_DV
    widths = (attn_w, kv_w, kv_w, IDX_HEADS * IDX_DIM, IDX_DIM, IDX_HEADS, gla_kw, gla_kw, gla_vw, GLA_RANK, gla_vw)
    assert sum(widths) == w_in.shape[1]
    offs = [0]
    for w in widths:
        offs.append(offs[-1] + w)
    col = lambda a, b: w_in[:, offs[a]:offs[b]].astype(BF16)
    wq, wkv, wqi = col(0, 1), col(1, 3), col(3, 4)
    wkw = jnp.pad(col(4, 6), ((0, 0), (0, LANES - IDX_DIM - IDX_HEADS)))
    wgq, wgk, wgv, wgr = col(6, 7), col(7, 8), col(8, 9), col(10, 11)
    wlr = jnp.pad(col(9, 10), ((0, 0), (0, LANES - GLA_RANK)))
    wa2 = jnp.pad(w_a2.astype(BF16), ((0, LANES - GLA_RANK), (0, 0)))
    return (wq, wqi, wkv, wkw), (wgq, wgk, wgv, wgr, wlr, wa2, b_a.reshape(1, -1).astype(F32))


def kernel(x_prompt, x_sample, cache_k, cache_v, cache_idx_k, state_gla, cache_mem_k, cache_mem_v, mem_prompt,
           rel_bias, ln_g, ln_b, ffn1_wg, ffn1_wu, ffn1_wd, w_in, w_a2, b_a, gla_norm_g, w_o, w_mq, w_mk, w_mv,
           w_mo, ffn2_wg, ffn2_wu, ffn2_wd):
    depth = w_in.shape[0]
    bp, tp, d = x_prompt.shape
    bs, ts, _ = x_sample.shape
    past = cache_k.shape[2]
    alpha = (2.0 * depth) ** 0.25
    attn_w = N_HEADS * HEAD_DIM
    kv_w = N_KV_HEADS * HEAD_DIM
    mem_w = MEM_HEADS * MEM_HEAD_DIM
    assert tp % LANES == 0 and ts == CHUNK and past % LANES == 0

    xp = x_prompt.reshape(bp * tp, d)
    xs = x_sample.reshape(bs * ts, d)
    pk, pv, pki, pS, pmk, pmv = [], [], [], [], [], []
    sk, sv, ski, sS = [], [], [], []
    bw_p = _bias_window(rel_bias, LANES)
    bw_s = bw_p.reshape(2, N_KV_HEADS, GROUP, LANES, 2 * LANES)[:, :, :, :ts].reshape(2, N_KV_HEADS, GROUP * ts, 2 * LANES)
    for l in range(depth):
        g = lambda i: ln_g[l, i].reshape(1, d)
        b = lambda i: ln_b[l, i].reshape(1, d)
        bf = lambda w: w[l].astype(BF16)
        f1 = (bf(ffn1_wg), bf(ffn1_wu), bf(ffn1_wd))
        f2 = (bf(ffn2_wg), bf(ffn2_wu), bf(ffn2_wd))
        w_attn, w_gla = _split_w_in(w_in[l], w_a2[l], b_a[l])
        woa, wog = w_o[l, :attn_w].astype(BF16), w_o[l, attn_w:].astype(BF16)
        gn = gla_norm_g[l].reshape(GLA_HEADS, 1, GLA_DV).astype(F32)

        xp = _ffn_ln(xp, *f1, g(0), b(0), alpha)
        xs = _ffn_ln(xs, *f1, g(0), b(0), alpha)

        q, qi, k, v, kb, vb, ki, ki2, wi = _attn_proj(xp, *w_attn)
        r3 = lambda a, n=bp, t=tp: a.reshape(n, t, a.shape[-1])
        attn = _dsa(r3(q), r3(qi), r3(wi), r3(kb), r3(vb), r3(ki2), bw_p,
                    tq=LANES, cbs=DSA_SCORE_TILES, cbf=DSA_FAR_TILES, qb0=0, n_sel=min(TOPK_MAX, tp // 4))
        gq, gk, gv, la, gr = _gla_proj(xp, *w_gla)
        s0 = jnp.zeros((bp, GLA_HEADS, GLA_DK, GLA_DV), state_gla.dtype)
        go, s_p = _gla(r3(gq), r3(gk), r3(gv), r3(la), r3(gr), gn, s0, chunk=CHUNK, n_chunks=8)
        xp = _out_ln(xp, attn.reshape(bp * tp, attn_w), go.reshape(bp * tp, -1), woa, wog, g(1), b(1), alpha)
        pk.append(k.reshape(bp, tp, N_KV_HEADS, HEAD_DIM))
        pv.append(v.reshape(bp, tp, N_KV_HEADS, HEAD_DIM))
        pki.append(ki.reshape(bp, tp, IDX_DIM))
        pS.append(s_p)

        q, qi, k, v, kb, vb, ki, ki2, wi = _attn_proj(xs, *w_attn)
        r3s = lambda a: a.reshape(bs, ts, a.shape[-1])
        pad_t = (-(past + ts)) % (LANES * max(DSA_SCORE_TILES, COUNT_UNROLL))
        cat = lambda c, n: jnp.pad(jnp.concatenate([c.astype(BF16), r3s(n)], axis=1), ((0, 0), (0, pad_t), (0, 0)))
        k_all = cat(cache_k[l].reshape(bs, past, kv_w), kb)
        v_all = cat(cache_v[l].reshape(bs, past, kv_w), vb)
        ki2_all = cat(jnp.concatenate([cache_idx_k[l]] * 2, axis=-1), ki2)
        attn = _dsa(r3s(q), r3s(qi), r3s(wi), k_all, v_all, ki2_all, bw_s,
                    tq=ts, cbs=DSA_SCORE_TILES, cbf=DSA_FAR_TILES, qb0=past // LANES,
                    n_sel=min(TOPK_MAX, (past + ts) // 4))
        gq, gk, gv, la, gr = _gla_proj(xs, *w_gla)
        go, s_s = _gla(r3s(gq), r3s(gk), r3s(gv), r3s(la), r3s(gr), gn, state_gla[l], chunk=ts, n_chunks=1)
        xs = _out_ln(xs, attn.reshape(bs * ts, attn_w), go.reshape(bs * ts, -1), woa, wog, g(1), b(1), alpha)
        sk.append(k.reshape(bs, ts, N_KV_HEADS, HEAD_DIM))
        sv.append(v.reshape(bs, ts, N_KV_HEADS, HEAD_DIM))
        ski.append(ki.reshape(bs, ts, IDX_DIM))
        sS.append(s_s)

        n_mem = mem_prompt.shape[1]
        mkv = _matmul(mem_prompt.reshape(bp * n_mem, d), jnp.concatenate([bf(w_mk), bf(w_mv)], axis=1))
        mk_p = mkv[:, :mem_w].reshape(bp, n_mem, mem_w)
        mv_p = mkv[:, mem_w:].reshape(bp, n_mem, mem_w)
        wmq, wmo = bf(w_mq), bf(w_mo)
        xp = _mem_ln(xp.reshape(bp, tp, d), wmq, mk_p.astype(BF16), mv_p.astype(BF16), wmo, g(2), b(2),
                     alpha).reshape(bp * tp, d)
        xs = _mem_ln(xs.reshape(bs, ts, d), wmq, cache_mem_k[l].reshape(bs, n_mem, mem_w).astype(BF16),
                     cache_mem_v[l].reshape(bs, n_mem, mem_w).astype(BF16), wmo, g(2), b(2),
                     alpha).reshape(bs * ts, d)
        pmk.append(mk_p.reshape(bp, n_mem, MEM_HEADS, MEM_HEAD_DIM))
        pmv.append(mv_p.reshape(bp, n_mem, MEM_HEADS, MEM_HEAD_DIM))

        xp = _ffn_ln(xp, *f2, g(3), b(3), alpha)
        xs = _ffn_ln(xs, *f2, g(3), b(3), alpha)

    return (xp.reshape(bp, tp, d), xs.reshape(bs, ts, d),
            jnp.stack(pk), jnp.stack(pv), jnp.stack(pki), jnp.stack(pS), jnp.stack(pmk), jnp.stack(pmv),
            jnp.stack(sk), jnp.stack(sv), jnp.stack(ski), jnp.stack(sS))
```
